```python
import math
import jax, jax.numpy as jnp
from jax import lax
import numpy as np

D_MODEL = 1024
BATCH = 1
SEQ = 16384
DEPTH = 2
DEC_BATCH = 16
DEC_SEQ = 64
PAST_LEN = 1024

CHUNK = 64
N_EVEN = (DEPTH + 1) // 2
N_ODD = DEPTH // 2
EPS = 1e-6
F32 = jnp.float32

SSD_D = D_MODEL
SSD_HEADDIM = 64
SSD_HEADS = SSD_D // SSD_HEADDIM
SSD_GROUPS = 2
SSD_STATE = 64
SSD_CONV = 4
SSD_CONV_DIM = SSD_D + 2 * SSD_GROUPS * SSD_STATE
SSD_COLS = SSD_D + SSD_CONV_DIM + SSD_HEADS
HG_HEADS = 8
HG_DK = 128
HG_DV = D_MODEL // HG_HEADS
HG_KD = HG_HEADS * HG_DK
HG_VD = HG_HEADS * HG_DV
HG_COLS = 2 * HG_KD + 2 * HG_VD
HG_BLOCK = 16
EVEN_COLS = SSD_COLS + HG_COLS
EVEN_MIX = SSD_D + HG_VD
RW_HEADS = 16
RW_N = 64
RW_D = RW_HEADS * RW_N
RW_DECAY_LORA = 64
RW_AAA_LORA = 64
RW_GATE_LORA = 128
RW_COLS = 3 * RW_D + RW_DECAY_LORA + RW_AAA_LORA + RW_GATE_LORA
RW_LN_EPS = 64e-5
FOX_HEADS = 16
FOX_HEADDIM = 64
FOX_D = FOX_HEADS * FOX_HEADDIM
FOX_COLS = 3 * FOX_D + FOX_HEADS
FOX_QBLOCK = 128
ODD_COLS = RW_COLS + FOX_COLS
ODD_MIX = RW_D + FOX_D
N_EXPERTS = 64
TOP_K = 6
N_EXPERT_GROUPS = 8
TOPK_GROUPS = 4
D_EXPERT = 256
D_SHARED = 256
ROUTED_SCALE = 2.5
MOE_BLOCK = 128

kernel_name = 'hybrid_ssd_hgrn2_rwkv7_fox_moe_stream_step'


def _split(x, sizes):
    return jnp.split(x, [int(s) for s in np.cumsum(sizes)[:-1]], axis=-1)


def rmsnorm(x, g, eps=EPS):
    xf = x.astype(F32)
    y = xf * lax.rsqrt(jnp.mean(xf * xf, axis=-1, keepdims=True) + eps)
    return (y * g.astype(F32)).astype(x.dtype)


def _to_blocks(t, L):
    pad = (-t.shape[1]) % L
    if pad:
        t = jnp.pad(t, [(0, 0), (0, pad)] + [(0, 0)] * (t.ndim - 2))
    b, tp = t.shape[:2]
    return jnp.moveaxis(t.reshape((b, tp // L, L) + t.shape[2:]), 1, 0)


def _from_blocks(t, T):
    t = jnp.moveaxis(t, 0, 1)
    return t.reshape((t.shape[0], -1) + t.shape[3:])[:, :T]


def causal_dwconv(u, past, w, b):
    T, W = u.shape[1], w.shape[0]
    full = jnp.concatenate([past.astype(u.dtype), u], axis=1)
    y = b
    for i in range(W):
        y = y + full[:, i:i + T] * w[i]
    return y, full[:, T:]


def ssd_scan(xs, bm, cm, a, h0):
    T = xs.shape[1]
    L = CHUNK
    tri = jnp.tril(jnp.ones((L, L), bool))

    def step(h, blk):
        x_c, b_c, c_c, a_c = blk
        cum = jnp.cumsum(a_c, axis=1)
        seg = cum[:, :, None, :] - cum[:, None, :, :]
        decay = jnp.exp(jnp.where(tri[None, :, :, None], seg, -jnp.inf))
        scores = jnp.einsum('bthn,bshn->btsh', c_c, b_c) * decay
        y = jnp.einsum('btsh,bshp->bthp', scores, x_c)
        y = y + jnp.einsum('bthn,bhpn->bthp', c_c * jnp.exp(cum)[..., None], h)
        last = cum[:, -1]
        xw = x_c * jnp.exp(last[:, None] - cum)[..., None]
        h = h * jnp.exp(last)[:, :, None, None] + jnp.einsum('bshp,bshn->bhpn', xw, b_c)
        return h, y

    blocks = tuple(_to_blocks(t, L) for t in (xs, bm, cm, a))
    h, ys = lax.scan(step, h0.astype(F32), blocks)
    return _from_blocks(ys, T), h


def gla_scan(q, k, v, logf, s0):
    T = q.shape[1]
    L = HG_BLOCK
    tri = jnp.tril(jnp.ones((L, L), bool))

    def step(s, blk):
        q_c, k_c, v_c, g_c = blk
        cum = jnp.cumsum(g_c, axis=1)
        seg = cum[:, :, None] - cum[:, None]
        decay = jnp.exp(jnp.where(tri[None, :, :, None, None], seg, -jnp.inf))
        scores = jnp.einsum('bthk,bshk,btshk->btsh', q_c, k_c, decay)
        o = jnp.einsum('btsh,bshv->bthv', scores, v_c)
        o = o + jnp.einsum('bthk,bhkv->bthv', q_c * jnp.exp(cum), s)
        last = cum[:, -1]
        kw = k_c * jnp.exp(last[:, None] - cum)
        s = s * jnp.exp(last)[..., None] + jnp.einsum('bshk,bshv->bhkv', kw, v_c)
        return s, o

    blocks = tuple(_to_blocks(t, L) for t in (q, k, v, logf))
    s, os_ = lax.scan(step, s0.astype(F32), blocks)
    return _from_blocks(os_, T), s


def rwkv7_scan(r, wlog, k, v, kk, a, s0):
    def step(s, inp):
        r_t, wl_t, k_t, v_t, kk_t, a_t = inp
        sa = jnp.einsum('bhvk,bhk->bhv', s, -kk_t)
        s = (s * jnp.exp(wl_t)[:, :, None, :] + sa[..., None] * (kk_t * a_t)[:, :, None, :]
             + v_t[..., None] * k_t[:, :, None, :])
        return s, jnp.einsum('bhvk,bhk->bhv', s, r_t)

    seq = tuple(jnp.moveaxis(t, 1, 0) for t in (r, wlog, k, v, kk, a))
    s, o = lax.scan(step, s0.astype(F32), seq)
    return jnp.moveaxis(o, 0, 1), s


def fox_attention(q, k, v, logf, k_past, v_past, logf_past):
    B_, T, H, Dh = q.shape
    P = k_past.shape[1]
    kc = jnp.concatenate([k_past.astype(k.dtype), k], axis=1)
    vc = jnp.concatenate([v_past.astype(v.dtype), v], axis=1)
    cum = jnp.cumsum(jnp.concatenate([logf_past.astype(F32), logf], axis=1), axis=1)
    cum_k = jnp.moveaxis(cum, 1, 2)
    cum_q = cum[:, P:]
    qb = FOX_QBLOCK if T % FOX_QBLOCK == 0 else T
    key_pos = jnp.arange(P + T)
    scale = Dh ** -0.5

    def q_block(i):
        start = i * qb
        q_b = lax.dynamic_slice_in_dim(q, start, qb, axis=1)
        cq = jnp.moveaxis(lax.dynamic_slice_in_dim(cum_q, start, qb, axis=1), 1, 2)
        s = jnp.einsum('bthd,bshd->bhts', q_b, kc).astype(F32) * scale
        s = s + (cq[..., None] - cum_k[:, :, None, :])
        q_pos = P + start + jnp.arange(qb)
        s = jnp.where(key_pos[None, :] <= q_pos[:, None], s, -jnp.inf)
        p = jax.nn.softmax(s, axis=-1)
        return jnp.einsum('bhts,bshd->bthd', p.astype(vc.dtype), vc)

    out = lax.map(q_block, jnp.arange(T // qb))
    return jnp.moveaxis(out, 0, 1).reshape(B_, T, H, Dh)


def even_mixer(h, prm, j, conv_state, ssd_state, hg_state):
    B_, T, _ = h.shape
    cdt = h.dtype
    proj = h @ prm['even_in_w'][j]
    z, xbc, dt_raw, hq, hf, hi, hgate = _split(proj, [SSD_D, SSD_CONV_DIM, SSD_HEADS, HG_KD, HG_KD, HG_VD, HG_VD])
    xbc, new_conv = causal_dwconv(xbc, conv_state, prm['ssd_conv_w'][j], prm['ssd_conv_b'][j])
    xbc = jax.nn.silu(xbc)
    xs, bm, cm = _split(xbc, [SSD_D, SSD_GROUPS * SSD_STATE, SSD_GROUPS * SSD_STATE])
    rep = SSD_HEADS // SSD_GROUPS
    xs = xs.reshape(B_, T, SSD_HEADS, SSD_HEADDIM).astype(F32)
    bm = jnp.repeat(bm.reshape(B_, T, SSD_GROUPS, SSD_STATE), rep, axis=2).astype(F32)
    cm = jnp.repeat(cm.reshape(B_, T, SSD_GROUPS, SSD_STATE), rep, axis=2).astype(F32)
    dt = jax.nn.softplus(dt_raw.astype(F32) + prm['ssd_dt_bias'][j].astype(F32))
    a_neg = -jnp.exp(prm['ssd_a_log'][j].astype(F32))
    y, new_ssd = ssd_scan(xs * dt[..., None], bm, cm, dt * a_neg, ssd_state)
    y = y + xs * prm['ssd_d'][j].astype(F32)[:, None]
    y = y.reshape(B_, T, SSD_D).astype(cdt)
    y_a = rmsnorm(y * jax.nn.silu(z), prm['ssd_norm_g'][j])
    lb = jnp.cumsum(jax.nn.softmax(prm['hgrn_lb_logits'].astype(F32), axis=0), axis=0)[j]
    f = lb + (1.0 - lb) * jax.nn.sigmoid(hf.astype(F32))
    q = jax.nn.silu(hq.astype(F32)) * HG_DK ** -0.5
    ks = (B_, T, HG_HEADS, HG_DK)
    o, new_hg = gla_scan(q.reshape(ks), (1.0 - f).reshape(ks),
                         hi.astype(F32).reshape(B_, T, HG_HEADS, HG_DV), jnp.log(f).reshape(ks), hg_state)
    o = rmsnorm(o, prm['hgrn_norm_g'][j].reshape(HG_HEADS, HG_DV)).reshape(B_, T, HG_VD).astype(cdt)
    y_b = o * jax.nn.silu(hgate)
    out = jnp.concatenate([y_a, y_b], axis=-1) @ prm['even_out_w'][j]
    return out, (new_conv, new_ssd, new_hg)


def odd_mixer(h, prm, j, shift_state, rw_state, k_past, v_past, logf_past):
    B_, T, _ = h.shape
    cdt = h.dtype
    proj = h @ prm['odd_in_w'][j]
    pr, pf = proj[..., :RW_COLS], proj[..., RW_COLS:]
    prev = jnp.concatenate([shift_state.astype(cdt)[:, None], pr[:, :-1]], axis=1)
    ps = pr + (prev - pr) * prm['rwkv_mu'][j]
    r, k, v, wd, ad, gd = _split(ps, [RW_D, RW_D, RW_D, RW_DECAY_LORA, RW_AAA_LORA, RW_GATE_LORA])
    w = -jax.nn.softplus(-(prm['rwkv_w0'][j] + jnp.tanh(wd) @ prm['rwkv_w2'][j])) - 0.5
    wlog = -jnp.exp(w.astype(F32))
    a = jax.nn.sigmoid(prm['rwkv_a0'][j] + ad @ prm['rwkv_a2'][j])
    g = jax.nn.sigmoid(gd) @ prm['rwkv_g2'][j]
    hs = (B_, T, RW_HEADS, RW_N)
    kk = (k * prm['rwkv_k_k'][j]).reshape(hs).astype(F32)
    kk = kk * lax.rsqrt(jnp.maximum(jnp.sum(kk * kk, axis=-1, keepdims=True), 1e-24))
    k = k * (1.0 + (a - 1.0) * prm['rwkv_k_a'][j])
    r_h, k_h, v_h, a_h = (t.reshape(hs).astype(F32) for t in (r, k, v, a))
    o, new_rw = rwkv7_scan(r_h, wlog.reshape(hs), k_h, v_h, kk, a_h, rw_state)
    mu = jnp.mean(o, axis=-1, keepdims=True)
    var = jnp.mean(jnp.square(o - mu), axis=-1, keepdims=True)
    o = ((o - mu) * lax.rsqrt(var + RW_LN_EPS)).reshape(B_, T, RW_D)
    o = o * prm['rwkv_ln_g'][j].astype(F32) + prm['rwkv_ln_b'][j].astype(F32)
    bonus = jnp.sum(r_h * k_h * prm['rwkv_r_k'][j].astype(F32), axis=-1, keepdims=True) * v_h
    y_c = (o + bonus.reshape(B_, T, RW_D)).astype(cdt) * g
    q, kf, vf, fl = _split(pf, [FOX_D, FOX_D, FOX_D, FOX_HEADS])
    fs = (B_, T, FOX_HEADS, FOX_HEADDIM)
    logf = jax.nn.log_sigmoid(fl.astype(F32) + prm['fox_f_bias'][j].astype(F32))
    kf = kf.reshape(fs)
    vf = vf.reshape(fs)
    y_d = fox_attention(q.reshape(fs), kf, vf, logf, k_past, v_past, logf_past).reshape(B_, T, FOX_D)
    out = jnp.concatenate([y_c, y_d.astype(cdt)], axis=-1) @ prm['odd_out_w'][j]
    return out, (pr[:, -1], new_rw, kf, vf, logf)


def routed_experts(x, eidx, wsel, w_gate, w_up, w_down):
    n, D = x.shape
    A = n * TOP_K
    e_flat = eidx.reshape(-1)
    tok_flat = jnp.repeat(jnp.arange(n, dtype=jnp.int32), TOP_K)
    order = jnp.argsort(e_flat)
    e_s, tok_s, w_s = e_flat[order], tok_flat[order], wsel.reshape(-1)[order]
    counts = jnp.zeros((N_EXPERTS,), jnp.int32).at[e_flat].add(1)
    padded = (counts + MOE_BLOCK - 1) // MOE_BLOCK * MOE_BLOCK
    pend = jnp.cumsum(padded)
    dest = (pend - padded)[e_s] + jnp.arange(A, dtype=jnp.int32) - (jnp.cumsum(counts) - counts)[e_s]
    nblk = -(-A // MOE_BLOCK) + N_EXPERTS
    xbuf = jnp.zeros((nblk * MOE_BLOCK, D), x.dtype).at[dest].set(x[tok_s])
    blk_e = jnp.minimum(jnp.searchsorted(pend, jnp.arange(nblk, dtype=jnp.int32) * MOE_BLOCK, side='right'),
                        N_EXPERTS - 1)

    def expert_block(args):
        xb, e = args
        return (jax.nn.silu(xb @ w_gate[e]) * (xb @ w_up[e])) @ w_down[e]

    ybuf = lax.map(expert_block, (xbuf.reshape(nblk, MOE_BLOCK, D), blk_e)).reshape(nblk * MOE_BLOCK, D)
    contrib = ybuf[dest] * w_s[:, None].astype(ybuf.dtype)
    return jax.ops.segment_sum(contrib, tok_s, num_segments=n)


def moe_ffn(h, prm, l):
    B_, T, D = h.shape
    x = h.reshape(B_ * T, D)
    n = x.shape[0]
    scores = jax.nn.sigmoid(x.astype(F32) @ prm['moe_router_w'][l].astype(F32))
    biased = scores + prm['moe_router_bias'][l].astype(F32)
    per_group = N_EXPERTS // N_EXPERT_GROUPS
    grp_score = lax.top_k(biased.reshape(n, N_EXPERT_GROUPS, per_group), 2)[0].sum(-1)
    _, gidx = lax.top_k(grp_score, TOPK_GROUPS)
    gmask = jax.nn.one_hot(gidx, N_EXPERT_GROUPS, dtype=F32).sum(1) > 0
    emask = jnp.repeat(gmask, per_group, axis=1)
    _, eidx = lax.top_k(jnp.where(emask, biased, -jnp.inf), TOP_K)
    wsel = jnp.take_along_axis(scores, eidx, axis=1)
    wsel = wsel / jnp.sum(wsel, axis=-1, keepdims=True) * ROUTED_SCALE
    routed = routed_experts(x, eidx, wsel, prm['moe_w_gate'][l], prm['moe_w_up'][l], prm['moe_w_down'][l])
    shared = (jax.nn.silu(x @ prm['moe_sh_gate'][l]) * (x @ prm['moe_sh_up'][l])) @ prm['moe_sh_down'][l]
    return (shared + routed.astype(shared.dtype)).reshape(B_, T, D)


def trunk(x, c, prm, conv_st, ssd_st, hg_st, shift_st, rw_st, fk_past, fv_past, fl_past):
    n_conv, n_ssd, n_hg = [], [], []
    n_shift, n_rw, n_fk, n_fv, n_fl = [], [], [], [], []
    c_act = jax.nn.silu(c)
    for l in range(DEPTH):
        j = l // 2
        mod = c_act @ prm['ada_w'][l] + prm['ada_b'][l]
        sh1, sc1, gt1, sh2, sc2, gt2 = jnp.split(mod[:, None, :], 6, axis=-1)
        hn = rmsnorm(x, prm['norm_mix_g'][l]) * (1.0 + sc1) + sh1
        if l % 2 == 0:
            out, (s_cv, s_sd, s_hg) = even_mixer(hn, prm, j, conv_st[j], ssd_st[j], hg_st[j])
            n_conv.append(s_cv)
            n_ssd.append(s_sd)
            n_hg.append(s_hg)
        else:
            out, (s_sh, s_rw, s_k, s_v, s_l) = odd_mixer(hn, prm, j, shift_st[j], rw_st[j],
                                                          fk_past[j], fv_past[j], fl_past[j])
            n_shift.append(s_sh)
            n_rw.append(s_rw)
            n_fk.append(s_k)
            n_fv.append(s_v)
            n_fl.append(s_l)
        x = x + gt1 * out
        hn = rmsnorm(x, prm['norm_ffn_g'][l]) * (1.0 + sc2) + sh2
        x = x + gt2 * moe_ffn(hn, prm, l)
    y = rmsnorm(x, prm['final_norm_g'])
    return (y, jnp.stack(n_conv), jnp.stack(n_ssd), jnp.stack(n_hg), jnp.stack(n_shift), jnp.stack(n_rw),
            jnp.stack(n_fk), jnp.stack(n_fv), jnp.stack(n_fl))


def setup_inputs(seed: int = 0) -> dict:
    key = jax.random.key(seed)
    ks = iter(jax.random.split(key, 64))

    def nrm(shape, scale):
        return jax.random.normal(next(ks), shape, F32) * scale

    def gain(shape):
        return 1.0 + nrm(shape, 0.05)

    D = D_MODEL
    dt0 = jnp.exp(jax.random.uniform(next(ks), (N_EVEN, SSD_HEADS), F32, math.log(1e-3), math.log(1e-1)))
    return {
        'x_prompt': nrm((BATCH, SEQ, D), 1.0),
        'x_sample': nrm((DEC_BATCH, DEC_SEQ, D), 1.0),
        'c_prompt': nrm((BATCH, D), 1.0),
        'c_sample': nrm((DEC_BATCH, D), 1.0),
        'state_ssd_conv': nrm((N_EVEN, DEC_BATCH, SSD_CONV - 1, SSD_CONV_DIM), 1.0),
        'state_ssd': nrm((N_EVEN, DEC_BATCH, SSD_HEADS, SSD_HEADDIM, SSD_STATE), 0.1),
        'state_hgrn': nrm((N_EVEN, DEC_BATCH, HG_HEADS, HG_DK, HG_DV), 0.3),
        'state_rwkv_shift': nrm((N_ODD, DEC_BATCH, RW_COLS), 1.0),
        'state_rwkv': nrm((N_ODD, DEC_BATCH, RW_HEADS, RW_N, RW_N), 0.1),
        'cache_fox_k': nrm((N_ODD, DEC_BATCH, PAST_LEN, FOX_HEADS, FOX_HEADDIM), 1.0),
        'cache_fox_v': nrm((N_ODD, DEC_BATCH, PAST_LEN, FOX_HEADS, FOX_HEADDIM), 1.0),
        'cache_fox_logf': jax.nn.log_sigmoid(2.0 + nrm((N_ODD, DEC_BATCH, PAST_LEN, FOX_HEADS), 1.0)),
        'ada_w': nrm((DEPTH, D, 6 * D), 0.5 * D ** -0.5),
        'ada_b': nrm((DEPTH, 6 * D), 0.02),
        'norm_mix_g': gain((DEPTH, D)),
        'norm_ffn_g': gain((DEPTH, D)),
        'final_norm_g': gain((D,)),
        'even_in_w': nrm((N_EVEN, D, EVEN_COLS), D ** -0.5),
        'even_out_w': nrm((N_EVEN, EVEN_MIX, D), EVEN_MIX ** -0.5),
        'ssd_conv_w': nrm((N_EVEN, SSD_CONV, SSD_CONV_DIM), SSD_CONV ** -0.5),
        'ssd_conv_b': nrm((N_EVEN, SSD_CONV_DIM), 0.02),
        'ssd_dt_bias': dt0 + jnp.log(-jnp.expm1(-dt0)),
        'ssd_a_log': jnp.log(jax.random.uniform(next(ks), (N_EVEN, SSD_HEADS), F32, 1.0, 16.0)),
        'ssd_d': gain((N_EVEN, SSD_HEADS)),
        'ssd_norm_g': gain((N_EVEN, SSD_D)),
        'hgrn_lb_logits': nrm((N_EVEN + 1, HG_KD), 0.1),
        'hgrn_norm_g': gain((N_EVEN, HG_VD)),
        'odd_in_w': nrm((N_ODD, D, ODD_COLS), D ** -0.5),
        'odd_out_w': nrm((N_ODD, ODD_MIX, D), ODD_MIX ** -0.5),
        'rwkv_mu': jax.random.uniform(next(ks), (N_ODD, RW_COLS), F32),
        'rwkv_w0': -2.0 + nrm((N_ODD, RW_D), 0.5),
        'rwkv_w2': nrm((N_ODD, RW_DECAY_LORA, RW_D), 0.1 * RW_DECAY_LORA ** -0.5),
        'rwkv_a0': nrm((N_ODD, RW_D), 0.1),
        'rwkv_a2': nrm((N_ODD, RW_AAA_LORA, RW_D), 0.5 * RW_AAA_LORA ** -0.5),
        'rwkv_g2': nrm((N_ODD, RW_GATE_LORA, RW_D), RW_GATE_LORA ** -0.5),
        'rwkv_k_k': 0.85 + nrm((N_ODD, RW_D), 0.05),
        'rwkv_k_a': gain((N_ODD, RW_D)),
        'rwkv_r_k': nrm((N_ODD, RW_HEADS, RW_N), 0.1),
        'rwkv_ln_g': gain((N_ODD, RW_D)),
        'rwkv_ln_b': nrm((N_ODD, RW_D), 0.02),
        'fox_f_bias': 2.0 + nrm((N_ODD, FOX_HEADS), 0.5),
        'moe_router_w': nrm((DEPTH, D, N_EXPERTS), D ** -0.5),
        'moe_router_bias': nrm((DEPTH, N_EXPERTS), 0.01),
        'moe_w_gate': nrm((DEPTH, N_EXPERTS, D, D_EXPERT), D ** -0.5),
        'moe_w_up': nrm((DEPTH, N_EXPERTS, D, D_EXPERT), D ** -0.5),
        'moe_w_down': nrm((DEPTH, N_EXPERTS, D_EXPERT, D), D_EXPERT ** -0.5),
        'moe_sh_gate': nrm((DEPTH, D, D_SHARED), D ** -0.5),
        'moe_sh_up': nrm((DEPTH, D, D_SHARED), D ** -0.5),
        'moe_sh_down': nrm((DEPTH, D_SHARED, D), D_SHARED ** -0.5),
    }


def reference(x_prompt, x_sample, c_prompt, c_sample, state_ssd_conv, state_ssd, state_hgrn,
              state_rwkv_shift, state_rwkv, cache_fox_k, cache_fox_v, cache_fox_logf,
              ada_w, ada_b, norm_mix_g, norm_ffn_g, final_norm_g,
              even_in_w, even_out_w, ssd_conv_w, ssd_conv_b, ssd_dt_bias, ssd_a_log, ssd_d, ssd_norm_g,
              hgrn_lb_logits, hgrn_norm_g,
              odd_in_w, odd_out_w, rwkv_mu, rwkv_w0, rwkv_w2, rwkv_a0, rwkv_a2, rwkv_g2, rwkv_k_k, rwkv_k_a,
              rwkv_r_k, rwkv_ln_g, rwkv_ln_b, fox_f_bias,
              moe_router_w, moe_router_bias, moe_w_gate, moe_w_up, moe_w_down, moe_sh_gate, moe_sh_up, moe_sh_down):
    prm = dict(ada_w=ada_w, ada_b=ada_b, norm_mix_g=norm_mix_g, norm_ffn_g=norm_ffn_g, final_norm_g=final_norm_g,
               even_in_w=even_in_w, even_out_w=even_out_w, ssd_conv_w=ssd_conv_w, ssd_conv_b=ssd_conv_b,
               ssd_dt_bias=ssd_dt_bias, ssd_a_log=ssd_a_log, ssd_d=ssd_d, ssd_norm_g=ssd_norm_g,
               hgrn_lb_logits=hgrn_lb_logits, hgrn_norm_g=hgrn_norm_g,
               odd_in_w=odd_in_w, odd_out_w=odd_out_w, rwkv_mu=rwkv_mu, rwkv_w0=rwkv_w0, rwkv_w2=rwkv_w2,
               rwkv_a0=rwkv_a0, rwkv_a2=rwkv_a2, rwkv_g2=rwkv_g2, rwkv_k_k=rwkv_k_k, rwkv_k_a=rwkv_k_a,
               rwkv_r_k=rwkv_r_k, rwkv_ln_g=rwkv_ln_g, rwkv_ln_b=rwkv_ln_b, fox_f_bias=fox_f_bias,
               moe_router_w=moe_router_w, moe_router_bias=moe_router_bias, moe_w_gate=moe_w_gate,
               moe_w_up=moe_w_up, moe_w_down=moe_w_down, moe_sh_gate=moe_sh_gate, moe_sh_up=moe_sh_up,
               moe_sh_down=moe_sh_down)
    bp = x_prompt.shape[0]
    z_conv = jnp.zeros((N_EVEN, bp, SSD_CONV - 1, SSD_CONV_DIM), x_prompt.dtype)
    z_ssd = jnp.zeros((N_EVEN, bp, SSD_HEADS, SSD_HEADDIM, SSD_STATE), F32)
    z_hg = jnp.zeros((N_EVEN, bp, HG_HEADS, HG_DK, HG_DV), F32)
    z_shift = jnp.zeros((N_ODD, bp, RW_COLS), x_prompt.dtype)
    z_rw = jnp.zeros((N_ODD, bp, RW_HEADS, RW_N, RW_N), F32)
    z_kv = jnp.zeros((N_ODD, bp, 0, FOX_HEADS, FOX_HEADDIM), x_prompt.dtype)
    z_lf = jnp.zeros((N_ODD, bp, 0, FOX_HEADS), F32)
    (y_prompt, p_conv, p_ssd, p_hgrn, p_shift, p_rwkv, p_fox_k, p_fox_v, p_fox_logf) = trunk(
        x_prompt, c_prompt, prm, z_conv, z_ssd, z_hg, z_shift, z_rw, z_kv, z_kv, z_lf)
    (y_sample, s_conv, s_ssd, s_hgrn, s_shift, s_rwkv, s_fox_k, s_fox_v, s_fox_logf) = trunk(
        x_sample, c_sample, prm, state_ssd_conv, state_ssd, state_hgrn, state_rwkv_shift, state_rwkv,
        cache_fox_k, cache_fox_v, cache_fox_logf)
    return (y_prompt, y_sample, p_conv, p_ssd, p_hgrn, p_shift, p_rwkv, p_fox_k, p_fox_v, p_fox_logf,
            s_conv, s_ssd, s_hgrn, s_shift, s_rwkv, s_fox_k, s_fox_v, s_fox_logf)
```

```python
import functools

import jax
import jax.numpy as jnp
from jax import lax
from jax.experimental import pallas as pl
from jax.experimental.pallas import tpu as pltpu

F32 = jnp.float32
BF16 = jnp.bfloat16
HI = lax.Precision.HIGHEST

D = 1024
EPS = 1e-6
LANES = 128
SUBLANES = 8

SSD_HEADS = 16
SSD_HD = 64
SSD_STATE = 64
SSD_CONV = 4
SSD_CONV_DIM = 1280
SSD_L = 64
HG_HEADS = 8
HG_DK = 128
HG_DV = 128
HG_L = 16
RW_HEADS = 16
RW_N = 64
RW_COLS = 3328
RW_LN_EPS = 64e-5
RW_L = 64
FOX_HEADS = 16
FOX_HD = 64
N_EXPERTS = 64
TOP_K = 6
N_GROUPS = 8
TOPK_GROUPS = 4
D_EXPERT = 256
ROUTED_SCALE = 2.5
PROJ_COLS = 6528
PROJ_TN = 2176

NEG_INF = float("-inf")


def _cparams(sem, vmem_mb=None):
    kw = dict(dimension_semantics=sem)
    if vmem_mb is not None:
        kw["vmem_limit_bytes"] = vmem_mb * 1024 * 1024
    return pltpu.CompilerParams(**kw)


def _mm(a, b):
    return jnp.dot(a.astype(BF16), b.astype(BF16), preferred_element_type=F32)


def _mm_nt(a, b):
    return lax.dot_general(a.astype(BF16), b.astype(BF16), (((1,), (1,)), ((), ())),
                           preferred_element_type=F32)


def _mm_hi(a, b):
    return jnp.dot(a, b, precision=HI, preferred_element_type=F32)


def _sigmoid(x):
    return 1.0 / (1.0 + jnp.exp(-x))


def _silu(x):
    return x * _sigmoid(x)


def _softplus(x):
    return jnp.maximum(x, 0.0) + jnp.log1p(jnp.exp(-jnp.abs(x)))


def _row_tiles(G, gs, target):
    if gs >= target:
        assert gs % target == 0
        return 1, target
    gb = max(1, min(G, target // gs))
    while G % gb:
        gb -= 1
    return gb, gs


def _ada_kernel(c_ref, w_ref, b_ref, o_ref):
    c = c_ref[...]
    o_ref[0] = _mm(_silu(c), w_ref[0]) + b_ref[0]


def ada_mod(c_all, ada_w, ada_b):
    depth = ada_w.shape[0]
    bp = c_all.shape[0]
    tn = 1024
    return pl.pallas_call(
        _ada_kernel,
        grid=(depth, 6 * D // tn),
        in_specs=[
            pl.BlockSpec((bp, D), lambda l, j: (0, 0)),
            pl.BlockSpec((1, D, tn), lambda l, j: (l, 0, j)),
            pl.BlockSpec((1, 1, tn), lambda l, j: (l, 0, j)),
        ],
        out_specs=pl.BlockSpec((1, bp, tn), lambda l, j: (l, 0, j)),
        out_shape=jax.ShapeDtypeStruct((depth, bp, 6 * D), F32),
        compiler_params=_cparams(("arbitrary", "arbitrary")),
        name="ada_mod",
    )(c_all, ada_w, ada_b.reshape(depth, 1, 6 * D))


def _rms_mod(x, g, sc, sh):
    y = x * lax.rsqrt(jnp.mean(x * x, axis=-1, keepdims=True) + EPS)
    return (y * g) * (1.0 + sc) + sh


def _inproj_kernel(x_ref, g_ref, sc_ref, sh_ref, w_ref, o_ref, xn_ref):
    @pl.when(pl.program_id(1) == 0)
    def _():
        hn = _rms_mod(x_ref[...], g_ref[...], sc_ref[...], sh_ref[...])
        xn_ref[...] = hn.reshape(xn_ref.shape).astype(BF16)

    o_ref[...] = jnp.dot(xn_ref[...], w_ref[...], preferred_element_type=F32)


def norm_inproj(x, g, sc, sh, w_bf):
    G, gs, _ = x.shape
    gb, tb = _row_tiles(G, gs, 512)
    nt = gs // tb
    tt = gb * tb
    n = G * gs
    out = pl.pallas_call(
        _inproj_kernel,
        grid=(n // tt, PROJ_COLS // PROJ_TN),
        in_specs=[
            pl.BlockSpec((gb, tb, D), lambda i, j: (i // nt, i % nt, 0)),
            pl.BlockSpec((1, D), lambda i, j: (0, 0)),
            pl.BlockSpec((gb, 1, D), lambda i, j: (i // nt, 0, 0)),
            pl.BlockSpec((gb, 1, D), lambda i, j: (i // nt, 0, 0)),
            pl.BlockSpec((D, PROJ_TN), lambda i, j: (0, j)),
        ],
        out_specs=pl.BlockSpec((tt, PROJ_TN), lambda i, j: (i, j)),
        out_shape=jax.ShapeDtypeStruct((n, PROJ_COLS), F32),
        scratch_shapes=[pltpu.VMEM((tt, D), BF16)],
        compiler_params=_cparams(("arbitrary", "arbitrary"), 48),
        name="norm_inproj",
    )(x, g.reshape(1, D), sc, sh, w_bf)
    return out.reshape(G, gs, PROJ_COLS)


def _outproj_kernel(ya_ref, yb_ref, w_ref, x_ref, gt_ref, g_ref, sc_ref, sh_ref, xo_ref, hn_ref):
    gb, tb, _ = x_ref.shape
    out = (jnp.dot(ya_ref[...].astype(BF16), w_ref[0:D, :], preferred_element_type=F32)
           + jnp.dot(yb_ref[...].astype(BF16), w_ref[D:2 * D, :], preferred_element_type=F32))
    x = x_ref[...] + gt_ref[...] * out.reshape(gb, tb, D)
    xo_ref[...] = x
    hn_ref[...] = _rms_mod(x, g_ref[...], sc_ref[...], sh_ref[...])


def mix_outproj(ya, yb, w_bf, x, gt, g, sc, sh):
    G, gs, _ = x.shape
    gb, tb = _row_tiles(G, gs, 512)
    nt = gs // tb
    tt = gb * tb
    n = G * gs
    row3 = pl.BlockSpec((gb, tb, D), lambda i: (i // nt, i % nt, 0))
    mod3 = pl.BlockSpec((gb, 1, D), lambda i: (i // nt, 0, 0))
    return pl.pallas_call(
        _outproj_kernel,
        grid=(n // tt,),
        in_specs=[
            pl.BlockSpec((tt, D), lambda i: (i, 0)),
            pl.BlockSpec((tt, D), lambda i: (i, 0)),
            pl.BlockSpec((2 * D, D), lambda i: (0, 0)),
            row3, mod3,
            pl.BlockSpec((1, D), lambda i: (0, 0)),
            mod3, mod3,
        ],
        out_specs=[row3, row3],
        out_shape=[jax.ShapeDtypeStruct((G, gs, D), F32), jax.ShapeDtypeStruct((G, gs, D), F32)],
        compiler_params=_cparams(("arbitrary",), 48),
        name="mix_outproj",
    )(ya.reshape(n, D), yb.reshape(n, D), w_bf, x, gt, g.reshape(1, D), sc, sh)


def _ssd_kernel(z_ref, xbc_ref, dt_ref, conv0_ref, st0_ref, cw_ref, cb_ref, dtb_ref, alog_ref, dfull_ref,
                ng_ref, y_ref, convo_ref, sto_ref, ext_ref, st_ref, yacc_ref):
    L = SSD_L
    c = pl.program_id(1)

    @pl.when(c == 0)
    def _():
        ext_ref[0:SUBLANES, :] = conv0_ref[0]
        st_ref[...] = st0_ref[0]

    u = xbc_ref[0]
    ext_ref[SUBLANES:SUBLANES + L, :] = u
    acc = cb_ref[...] + u * cw_ref[SSD_CONV - 1:SSD_CONV, :]
    for i in range(SSD_CONV - 1):
        sh = SSD_CONV - 1 - i
        acc = acc + ext_ref[SUBLANES - sh:SUBLANES - sh + L, :] * cw_ref[i:i + 1, :]
    ext_ref[0:SUBLANES, :] = ext_ref[L:L + SUBLANES, :]
    xbc = _silu(acc)
    xs = xbc[:, 0:D]
    bm2 = xbc[:, D:D + LANES]
    cm2 = xbc[:, D + LANES:D + 2 * LANES]

    dt = _softplus(dt_ref[0] + dtb_ref[...])
    a = dt * (-jnp.exp(alog_ref[...]))
    row = lax.broadcasted_iota(jnp.int32, (L, L), 0)
    col = lax.broadcasted_iota(jnp.int32, (L, L), 1)
    tril = (col <= row).astype(F32)
    triu = (row <= col).astype(F32)
    cum = _mm_hi(tril, a)
    cum_t = _mm_hi(a.T, triu)
    eh = lax.broadcasted_iota(jnp.int32, (LANES, D), 0)
    ec = lax.broadcasted_iota(jnp.int32, (LANES, D), 1)
    expand = (ec // SSD_HD == eh).astype(F32)
    dt_full = _mm_hi(dt, expand)
    cum_full = _mm_hi(cum, expand)
    ecum_full = jnp.exp(cum_full)
    last_full = cum_full[L - 1:L, :]
    w_full = jnp.exp(last_full - cum_full)
    xdt = xs * dt_full

    lane = lax.broadcasted_iota(jnp.int32, (L, LANES), 1)
    lo = lane < SSD_HD
    tri2 = (lane % L) <= lax.broadcasted_iota(jnp.int32, (L, LANES), 0)
    bd_r = lax.broadcasted_iota(jnp.int32, (LANES, LANES), 0) // SSD_HD
    bd_c = lax.broadcasted_iota(jnp.int32, (LANES, LANES), 1) // SSD_HD
    blockdiag = bd_r == bd_c
    bm_r = pltpu.roll(bm2, SSD_HD, 1)
    cm_r = pltpu.roll(cm2, SSD_HD, 1)
    bmstack = jnp.concatenate([bm2, bm2], axis=0)

    for g in range(2):
        grp_lo = lo if g == 0 else jnp.logical_not(lo)
        cm_m = jnp.where(grp_lo, cm2, 0.0)
        cb_dup = _mm_nt(cm_m, bmstack)
        cm_dup = jnp.where(grp_lo, cm2, cm_r)
        bm_dup = jnp.where(grp_lo, bm2, bm_r)
        for pp in range(4):
            p = 4 * g + pp
            sl = slice(LANES * p, LANES * (p + 1))
            cum_pair = cum_full[:, sl]
            cum_row = jnp.concatenate([cum_t[2 * p:2 * p + 1, :], cum_t[2 * p + 1:2 * p + 2, :]], axis=1)
            s2 = cb_dup * jnp.exp(jnp.where(tri2, cum_pair - cum_row, NEG_INF))
            x2 = xdt[:, sl]
            xstack = jnp.concatenate([jnp.where(lo, x2, 0.0), jnp.where(lo, 0.0, x2)], axis=0)
            hp = st_ref[p]
            lhs = jnp.concatenate([s2, cm_dup * ecum_full[:, sl]], axis=1)
            rhs = jnp.concatenate([xstack, hp], axis=0)
            yacc_ref[:, sl] = _mm(lhs, rhs)
            bw_t = (bm_dup * w_full[:, sl]).T
            upd = _mm(bw_t, x2)
            st_ref[p] = jnp.where(blockdiag, hp * ecum_full[L - 1:L, sl] + upd, 0.0)

    y = yacc_ref[...] + xs * dfull_ref[...]
    z = z_ref[0]
    yz = y * _silu(z)
    y_ref[0] = yz * lax.rsqrt(jnp.mean(yz * yz, axis=-1, keepdims=True) + EPS) * ng_ref[...]

    @pl.when(c == pl.num_programs(1) - 1)
    def _():
        convo_ref[0] = ext_ref[0:SUBLANES, :]
        sto_ref[0] = st_ref[...]


def ssd_mixer(proj, conv_state, ssd_state, conv_w, conv_b, dt_bias, a_log, d_skip, norm_g):
    B, T, _ = proj.shape
    L = SSD_L
    assert T % L == 0
    H, P, N = SSD_HEADS, SSD_HD, SSD_STATE
    conv0 = jnp.pad(conv_state, ((0, 0), (SUBLANES - (SSD_CONV - 1), 0), (0, 0)))
    st_t = jnp.swapaxes(ssd_state, -1, -2).reshape(B, H // 2, 2, N, P)
    st0 = jnp.zeros((B, H // 2, 2, N, 2, P), F32)
    st0 = st0.at[:, :, 0, :, 0, :].set(st_t[:, :, 0]).at[:, :, 1, :, 1, :].set(st_t[:, :, 1])
    st0 = st0.reshape(B, H // 2, 2 * N, 2 * P)
    cw = jnp.pad(conv_w, ((0, SUBLANES - SSD_CONV), (0, 0)))
    pad16 = lambda v: jnp.pad(v.reshape(1, H), ((0, 0), (0, LANES - H)))
    dfull = jnp.repeat(d_skip, P).reshape(1, D)
    vec = lambda w: pl.BlockSpec((1, w), lambda b, c: (0, 0))
    y, convo, sto = pl.pallas_call(
        _ssd_kernel,
        grid=(B, T // L),
        in_specs=[
            pl.BlockSpec((1, L, D), lambda b, c: (b, c, 0)),
            pl.BlockSpec((1, L, SSD_CONV_DIM), lambda b, c: (b, c, 4)),
            pl.BlockSpec((1, L, LANES), lambda b, c: (b, c, 50)),
            pl.BlockSpec((1, SUBLANES, SSD_CONV_DIM), lambda b, c: (b, 0, 0)),
            pl.BlockSpec((1, H // 2, LANES, LANES), lambda b, c: (b, 0, 0, 0)),
            pl.BlockSpec((SUBLANES, SSD_CONV_DIM), lambda b, c: (0, 0)),
            vec(SSD_CONV_DIM), vec(LANES), vec(LANES), vec(D), vec(D),
        ],
        out_specs=[
            pl.BlockSpec((1, L, D), lambda b, c: (b, c, 0)),
            pl.BlockSpec((1, SUBLANES, SSD_CONV_DIM), lambda b, c: (b, 0, 0)),
            pl.BlockSpec((1, H // 2, LANES, LANES), lambda b, c: (b, 0, 0, 0)),
        ],
        out_shape=[
            jax.ShapeDtypeStruct((B, T, D), F32),
            jax.ShapeDtypeStruct((B, SUBLANES, SSD_CONV_DIM), F32),
            jax.ShapeDtypeStruct((B, H // 2, LANES, LANES), F32),
        ],
        scratch_shapes=[
            pltpu.VMEM((SUBLANES + L, SSD_CONV_DIM), F32),
            pltpu.VMEM((H // 2, LANES, LANES), F32),
            pltpu.VMEM((L, D), F32),
        ],
        compiler_params=_cparams(("arbitrary", "arbitrary"), 48),
        name="ssd_mixer",
    )(proj, proj, proj, conv0, st0, cw, conv_b.reshape(1, SSD_CONV_DIM), pad16(dt_bias), pad16(a_log),
      dfull, norm_g.reshape(1, D))
    sto = sto.reshape(B, H // 2, 2, N, 2, P)
    new_st = jnp.stack([sto[:, :, 0, :, 0, :], sto[:, :, 1, :, 1, :]], axis=2).reshape(B, H, N, P)
    return y, convo[:, SUBLANES - (SSD_CONV - 1):], jnp.swapaxes(new_st, -1, -2)


def _hgrn_kernel(q_ref, f_ref, i_ref, gate_ref, st0_ref, lb_ref, ng_ref, y_ref, sto_ref, st_ref):
    L = HG_L
    c = pl.program_id(1)

    @pl.when(c == 0)
    def _():
        st_ref[...] = st0_ref[0]

    lb = lb_ref[...]
    f = lb + (1.0 - lb) * _sigmoid(f_ref[0])
    logf = jnp.log(f)
    q = _silu(q_ref[0]) * (HG_DK ** -0.5)
    k = 1.0 - f
    v = i_ref[0]
    row = lax.broadcasted_iota(jnp.int32, (L, L), 0)
    col = lax.broadcasted_iota(jnp.int32, (L, L), 1)
    cum = _mm_hi((col <= row).astype(F32), logf)
    qe = q * jnp.exp(cum)
    last = cum[L - 1:L, :]
    kw = k * jnp.exp(last - cum)
    elast = jnp.exp(last)
    srow = lax.broadcasted_iota(jnp.int32, (L, HG_DK), 0)
    for h in range(HG_HEADS):
        sl = slice(HG_DK * h, HG_DK * (h + 1))
        gh, qh, kh, vh = cum[:, sl], q[:, sl], k[:, sl], v[:, sl]
        st = st_ref[h]
        inter = _mm_nt(qe[:, sl], st)
        rows = []
        for t in range(L):
            dec = jnp.exp(jnp.where(srow <= t, gh[t:t + 1, :] - gh, NEG_INF))
            sc = jnp.sum(qh[t:t + 1, :] * kh * dec, axis=1, keepdims=True)
            rows.append(jnp.sum(sc * vh, axis=0, keepdims=True))
        o = inter + jnp.concatenate(rows, axis=0)
        st_ref[h] = st * elast[:, sl] + _mm(vh.T, kw[:, sl])
        on = o * lax.rsqrt(jnp.mean(o * o, axis=-1, keepdims=True) + EPS) * ng_ref[:, sl]
        y_ref[0, :, sl] = on * _silu(gate_ref[0, :, sl])

    @pl.when(c == pl.num_programs(1) - 1)
    def _():
        sto_ref[0] = st_ref[...]


def hgrn_mixer(proj, hg_state, lb, norm_g):
    B, T, _ = proj.shape
    L = HG_L
    assert T % L == 0
    st0 = jnp.swapaxes(hg_state, -1, -2)
    col = lambda j: pl.BlockSpec((1, L, D), lambda b, c: (b, c, j))
    st_spec = pl.BlockSpec((1, HG_HEADS, HG_DV, HG_DK), lambda b, c: (b, 0, 0, 0))
    vec = pl.BlockSpec((1, D), lambda b, c: (0, 0))
    y, sto = pl.pallas_call(
        _hgrn_kernel,
        grid=(B, T // L),
        in_specs=[col(1), col(2), col(3), col(4), st_spec, vec, vec],
        out_specs=[pl.BlockSpec((1, L, D), lambda b, c: (b, c, 0)), st_spec],
        out_shape=[jax.ShapeDtypeStruct((B, T, D), F32),
                   jax.ShapeDtypeStruct((B, HG_HEADS, HG_DV, HG_DK), F32)],
        scratch_shapes=[pltpu.VMEM((HG_HEADS, HG_DV, HG_DK), F32)],
        compiler_params=_cparams(("arbitrary", "arbitrary")),
        name="hgrn_mixer",
    )(proj, proj, proj, proj, st0, lb.reshape(1, D), norm_g.reshape(1, D))
    return y, jnp.swapaxes(sto, -1, -2)


def _first_max_onehot(vals, iota, axis, size):
    m = jnp.max(vals, axis=axis, keepdims=True)
    idx = jnp.min(jnp.where(vals == m, iota, size), axis=axis, keepdims=True)
    return iota == idx, m


def _router_kernel(x_ref, wt_ref, bias_ref, o_ref):
    tt = x_ref.shape[0]
    per = N_EXPERTS // N_GROUPS
    logits = lax.dot_general(wt_ref[...], x_ref[...], (((1,), (1,)), ((), ())),
                             precision=HI, preferred_element_type=F32)
    scores = _sigmoid(logits)
    biased = scores + bias_ref[...]
    b3 = biased.reshape(N_GROUPS, per, tt)
    i3 = lax.broadcasted_iota(jnp.int32, (N_GROUPS, per, tt), 1)
    oh, m1 = _first_max_onehot(b3, i3, 1, per)
    m2 = jnp.max(jnp.where(oh, NEG_INF, b3), axis=1, keepdims=True)
    grp = (m1 + m2).reshape(N_GROUPS, tt)
    ig = lax.broadcasted_iota(jnp.int32, (N_GROUPS, tt), 0)
    gsel = jnp.zeros((N_GROUPS, tt), jnp.bool_)
    for _ in range(TOPK_GROUPS):
        oh, _m = _first_max_onehot(grp, ig, 0, N_GROUPS)
        gsel = jnp.logical_or(gsel, oh)
        grp = jnp.where(oh, NEG_INF, grp)
    gmask = jnp.where(gsel, 1.0, 0.0).reshape(N_GROUPS, 1, tt)
    masked = jnp.where(gmask > 0.0, b3, NEG_INF).reshape(N_EXPERTS, tt)
    ie = lax.broadcasted_iota(jnp.int32, (N_EXPERTS, tt), 0)
    esel = jnp.zeros((N_EXPERTS, tt), F32)
    for _ in range(TOP_K):
        oh, _m = _first_max_onehot(masked, ie, 0, N_EXPERTS)
        esel = jnp.where(oh, 1.0, esel)
        masked = jnp.where(oh, NEG_INF, masked)
    wsel = scores * esel
    gates = wsel / jnp.sum(wsel, axis=0, keepdims=True) * ROUTED_SCALE
    o_ref[...] = gates.T


def moe_router(hn2d, router_w, router_bias):
    n = hn2d.shape[0]
    tt = min(n, 512)
    return pl.pallas_call(
        _router_kernel,
        grid=(n // tt,),
        in_specs=[
            pl.BlockSpec((tt, D), lambda i: (i, 0)),
            pl.BlockSpec((N_EXPERTS, D), lambda i: (0, 0)),
            pl.BlockSpec((N_EXPERTS, 1), lambda i: (0, 0)),
        ],
        out_specs=pl.BlockSpec((tt, N_EXPERTS), lambda i: (i, 0)),
        out_shape=jax.ShapeDtypeStruct((n, N_EXPERTS), F32),
        compiler_params=_cparams(("arbitrary",)),
        name="moe_router",
    )(hn2d, router_w.T, router_bias.reshape(N_EXPERTS, 1))


def _ffn(xb, wg, wu, wd, gate_col):
    h = _silu(jnp.dot(xb, wg.astype(BF16), preferred_element_type=F32)) * jnp.dot(
        xb, wu.astype(BF16), preferred_element_type=F32)
    if gate_col is not None:
        h = h * gate_col
    return jnp.dot(h.astype(BF16), wd.astype(BF16), preferred_element_type=F32)


def _moe_kernel(x_ref, gates_ref, wg_ref, wu_ref, wd_ref, sg_ref, su_ref, sd_ref, o_ref, xb_ref, acc_ref):
    e = pl.program_id(1)

    @pl.when(e == 0)
    def _():
        xb_ref[...] = x_ref[...].astype(BF16)
        acc_ref[...] = _ffn(xb_ref[...], sg_ref[0], su_ref[0], sd_ref[0], None)

    gates = gates_ref[...]
    lane = lax.broadcasted_iota(jnp.int32, gates.shape, 1)
    gate_col = jnp.sum(jnp.where(lane == e, gates, 0.0), axis=1, keepdims=True)
    acc_ref[...] += _ffn(xb_ref[...], wg_ref[0, 0], wu_ref[0, 0], wd_ref[0, 0], gate_col)

    @pl.when(e == pl.num_programs(1) - 1)
    def _():
        o_ref[...] = acc_ref[...]


def moe_experts(hn2d, gates, l, w_gate, w_up, w_down, sh_gate, sh_up, sh_down):
    n = hn2d.shape[0]
    tt = min(n, 1024)
    return pl.pallas_call(
        _moe_kernel,
        grid=(n // tt, N_EXPERTS),
        in_specs=[
            pl.BlockSpec((tt, D), lambda i, e: (i, 0)),
            pl.BlockSpec((tt, N_EXPERTS), lambda i, e: (i, 0)),
            pl.BlockSpec((1, 1, D, D_EXPERT), lambda i, e: (l, e, 0, 0)),
            pl.BlockSpec((1, 1, D, D_EXPERT), lambda i, e: (l, e, 0, 0)),
            pl.BlockSpec((1, 1, D_EXPERT, D), lambda i, e: (l, e, 0, 0)),
            pl.BlockSpec((1, D, D_EXPERT), lambda i, e: (l, 0, 0)),
            pl.BlockSpec((1, D, D_EXPERT), lambda i, e: (l, 0, 0)),
            pl.BlockSpec((1, D_EXPERT, D), lambda i, e: (l, 0, 0)),
        ],
        out_specs=pl.BlockSpec((tt, D), lambda i, e: (i, 0)),
        out_shape=jax.ShapeDtypeStruct((n, D), F32),
        scratch_shapes=[pltpu.VMEM((tt, D), BF16), pltpu.VMEM((tt, D), F32)],
        compiler_params=_cparams(("arbitrary", "arbitrary"), 56),
        name="moe_experts",
    )(hn2d, gates, w_gate, w_up, w_down, sh_gate, sh_up, sh_down)


def _residual_kernel(x_ref, f_ref, gt_ref, g_ref, o_ref, *, final_norm):
    x = x_ref[...] + gt_ref[...] * f_ref[...]
    if final_norm:
        x = x * lax.rsqrt(jnp.mean(x * x, axis=-1, keepdims=True) + EPS) * g_ref[...]
    o_ref[...] = x


def residual(x, ffn, gt, g, final_norm):
    G, gs, _ = x.shape
    gb, tb = _row_tiles(G, gs, 512)
    nt = gs // tb
    row3 = pl.BlockSpec((gb, tb, D), lambda i: (i // nt, i % nt, 0))
    return pl.pallas_call(
        functools.partial(_residual_kernel, final_norm=final_norm),
        grid=(G * gs // (gb * tb),),
        in_specs=[row3, row3, pl.BlockSpec((gb, 1, D), lambda i: (i // nt, 0, 0)),
                  pl.BlockSpec((1, D), lambda i: (0, 0))],
        out_specs=row3,
        out_shape=jax.ShapeDtypeStruct((G, gs, D), F32),
        compiler_params=_cparams(("arbitrary",)),
        name="residual",
    )(x, ffn.reshape(G, gs, D), gt, g.reshape(1, D))


def _head_sum(x):
    L = x.shape[0]
    lo = lax.broadcasted_iota(jnp.int32, (L, LANES), 1) < RW_N
    outs = []
    for p in range(RW_HEADS // 2):
        x2 = x[:, LANES * p:LANES * (p + 1)]
        s_lo = jnp.sum(jnp.where(lo, x2, 0.0), axis=1, keepdims=True)
        s_hi = jnp.sum(jnp.where(lo, 0.0, x2), axis=1, keepdims=True)
        outs.append(jnp.where(lo, s_lo, s_hi))
    return jnp.concatenate(outs, axis=1)


def _rwkv_kernel(pr_ref, shift0_ref, st0_ref, mu_ref, w0_ref, w2_ref, a0_ref, a2_ref, g2_ref, kk_ref, ka_ref,
                 rk_ref, lng_ref, lnb_ref, y_ref, shifto_ref, sto_ref,
                 ext_ref, st_ref, r_s, w_s, k_s, b_s, nkk_s, vt_s, ot_s, o_s):
    L = RW_L
    H, N = RW_HEADS, RW_N
    c = pl.program_id(1)

    @pl.when(c == 0)
    def _():
        ext_ref[0:SUBLANES, :] = shift0_ref[0]
        st_ref[...] = st0_ref[0]

    pr = pr_ref[0]
    ext_ref[SUBLANES:SUBLANES + L, :] = pr
    prev = ext_ref[SUBLANES - 1:SUBLANES - 1 + L, :]
    ext_ref[0:SUBLANES, :] = ext_ref[L:L + SUBLANES, :]
    ps = pr + (prev - pr) * mu_ref[...]
    r = ps[:, 0:D]
    k = ps[:, D:2 * D]
    v = ps[:, 2 * D:3 * D]
    wa = ps[:, 3 * D:3 * D + LANES]
    gd = ps[:, 3 * D + LANES:3 * D + 2 * LANES]
    w = -_softplus(-(w0_ref[...] + _mm(jnp.tanh(wa), w2_ref[...]))) - 0.5
    wdec = jnp.exp(-jnp.exp(w))
    a = _sigmoid(a0_ref[...] + _mm(wa, a2_ref[...]))
    g = _mm(_sigmoid(gd), g2_ref[...])
    kk = k * kk_ref[...]
    kk = kk * lax.rsqrt(jnp.maximum(_head_sum(kk * kk), 1e-24))
    k = k * (1.0 + (a - 1.0) * ka_ref[...])
    bonus = _head_sum(r * k * rk_ref[...]) * v
    b = kk * a
    for h in range(H):
        sl = slice(N * h, N * (h + 1))
        r_s[h] = r[:, sl]
        w_s[h] = wdec[:, sl]
        k_s[h] = k[:, sl]
        b_s[h] = b[:, sl]
        nkk_s[h] = -kk[:, sl]
        vt_s[h] = v[:, sl].T
    ot_s[...] = jnp.zeros(ot_s.shape, F32)

    def step(t, carry):
        lane_t = lax.broadcasted_iota(jnp.int32, (N, L), 1) == t
        for h in range(H):
            row = lambda ref: ref[h, pl.ds(t, 1), :]
            s = st_ref[h]
            sa = jnp.sum(s * row(nkk_s), axis=1, keepdims=True)
            vcol = jnp.sum(jnp.where(lane_t, vt_s[h], 0.0), axis=1, keepdims=True)
            s = s * row(w_s) + sa * row(b_s) + vcol * row(k_s)
            st_ref[h] = s
            ocol = jnp.sum(s * row(r_s), axis=1, keepdims=True)
            ot_s[h] = jnp.where(lane_t, ocol, ot_s[h])
        return carry

    lax.fori_loop(0, L, step, 0)

    for h in range(H):
        ot = ot_s[h]
        mean = jnp.mean(ot, axis=0, keepdims=True)
        cen = ot - mean
        var = jnp.mean(cen * cen, axis=0, keepdims=True)
        o_s[:, N * h:N * (h + 1)] = (cen * lax.rsqrt(var + RW_LN_EPS)).T
    o = o_s[...] * lng_ref[...] + lnb_ref[...]
    y_ref[0] = (o + bonus) * g

    @pl.when(c == pl.num_programs(1) - 1)
    def _():
        shifto_ref[0] = ext_ref[0:SUBLANES, :]
        sto_ref[0] = st_ref[...]


def rwkv_mixer(proj, shift_state, rw_state, mu, w0, w2, a0, a2, g2, k_k, k_a, r_k, ln_g, ln_b):
    B, T, _ = proj.shape
    L = RW_L
    assert T % L == 0
    H, N = RW_HEADS, RW_N
    shift0 = jnp.pad(shift_state[:, None, :], ((0, 0), (SUBLANES - 1, 0), (0, 0)))
    w2p = jnp.pad(w2, ((0, LANES - w2.shape[0]), (0, 0)))
    a2p = jnp.pad(a2, ((LANES - a2.shape[0], 0), (0, 0)))
    vec = lambda w: pl.BlockSpec((1, w), lambda b, c: (0, 0))
    mat = pl.BlockSpec((LANES, D), lambda b, c: (0, 0))
    st_spec = pl.BlockSpec((1, H, N, N), lambda b, c: (b, 0, 0, 0))
    sh_spec = pl.BlockSpec((1, SUBLANES, RW_COLS), lambda b, c: (b, 0, 0))
    rows = pltpu.VMEM((H, L, N), F32)
    cols = pltpu.VMEM((H, N, L), F32)
    y, shifto, sto = pl.pallas_call(
        _rwkv_kernel,
        grid=(B, T // L),
        in_specs=[pl.BlockSpec((1, L, RW_COLS), lambda b, c: (b, c, 0)), sh_spec, st_spec,
                  vec(RW_COLS), vec(D), mat, vec(D), mat, mat, vec(D), vec(D), vec(D), vec(D), vec(D)],
        out_specs=[pl.BlockSpec((1, L, D), lambda b, c: (b, c, 0)), sh_spec, st_spec],
        out_shape=[jax.ShapeDtypeStruct((B, T, D), F32),
                   jax.ShapeDtypeStruct((B, SUBLANES, RW_COLS), F32),
                   jax.ShapeDtypeStruct((B, H, N, N), F32)],
        scratch_shapes=[pltpu.VMEM((SUBLANES + L, RW_COLS), F32), pltpu.VMEM((H, N, N), F32),
                        rows, rows, rows, rows, rows, cols, cols, pltpu.VMEM((L, D), F32)],
        compiler_params=_cparams(("arbitrary", "arbitrary"), 48),
        name="rwkv_mixer",
    )(proj, shift0, rw_state, mu.reshape(1, RW_COLS), w0.reshape(1, D), w2p, a0.reshape(1, D), a2p, g2,
      k_k.reshape(1, D), k_a.reshape(1, D), r_k.reshape(1, D), ln_g.reshape(1, D), ln_b.reshape(1, D))
    return y, shifto[:, SUBLANES - 1], sto


def _logf_kernel(fl_ref, bias_ref, o_ref):
    x = fl_ref[0] + bias_ref[...]
    lf = jnp.minimum(x, 0.0) - jnp.log1p(jnp.exp(-jnp.abs(x)))
    o_ref[0] = lf[:, 0:FOX_HEADS]


def fox_logf(proj, f_bias):
    B, T, _ = proj.shape
    tt = min(T, 512)
    return pl.pallas_call(
        _logf_kernel,
        grid=(B, T // tt),
        in_specs=[pl.BlockSpec((1, tt, LANES), lambda b, i: (b, i, 50)),
                  pl.BlockSpec((1, LANES), lambda b, i: (0, 0))],
        out_specs=pl.BlockSpec((1, tt, FOX_HEADS), lambda b, i: (b, i, 0)),
        out_shape=jax.ShapeDtypeStruct((B, T, FOX_HEADS), F32),
        compiler_params=_cparams(("arbitrary", "arbitrary")),
        name="fox_logf",
    )(proj, jnp.pad(f_bias.reshape(1, FOX_HEADS), ((0, 0), (0, LANES - FOX_HEADS))))


def _cumsum_kernel(x_ref, o_ref, carry_ref):
    ts = x_ref.shape[1]

    @pl.when(pl.program_id(1) == 0)
    def _():
        carry_ref[...] = jnp.zeros(carry_ref.shape, F32)

    row = lax.broadcasted_iota(jnp.int32, (ts, ts), 0)
    col = lax.broadcasted_iota(jnp.int32, (ts, ts), 1)
    cum = _mm_hi((col <= row).astype(F32), x_ref[0]) + carry_ref[...]
    o_ref[0] = cum
    carry_ref[...] = cum[ts - 1:ts, :]


def time_cumsum(x):
    B, S, H = x.shape
    ts = 512 if S % 512 == 0 else LANES
    assert S % ts == 0
    return pl.pallas_call(
        _cumsum_kernel,
        grid=(B, S // ts),
        in_specs=[pl.BlockSpec((1, ts, H), lambda b, i: (b, i, 0))],
        out_specs=pl.BlockSpec((1, ts, H), lambda b, i: (b, i, 0)),
        out_shape=jax.ShapeDtypeStruct((B, S, H), F32),
        scratch_shapes=[pltpu.VMEM((1, H), F32)],
        compiler_params=_cparams(("arbitrary", "arbitrary")),
        name="time_cumsum",
    )(x)


def _fox_kernel(q_ref, k_ref, v_ref, cq_ref, ck_ref, o_ref, m_ref, l_ref, acc_ref, *, past, tq, tk):
    p = pl.program_id(1)
    i = pl.program_id(2)
    j = pl.program_id(3)

    @pl.when(j == 0)
    def _():
        m_ref[...] = jnp.full(m_ref.shape, NEG_INF, F32)
        l_ref[...] = jnp.zeros(l_ref.shape, F32)
        acc_ref[...] = jnp.zeros(acc_ref.shape, F32)

    lo = lax.broadcasted_iota(jnp.int32, (tq, LANES), 1) < FOX_HD

    @pl.when(j * tk <= past + (i + 1) * tq - 1)
    def _():
        q2 = q_ref[0] * (FOX_HD ** -0.5)
        k2 = k_ref[0].astype(BF16)
        v2 = v_ref[0].astype(BF16)
        q_pos = past + i * tq + lax.broadcasted_iota(jnp.int32, (tq, tk), 0)
        k_pos = j * tk + lax.broadcasted_iota(jnp.int32, (tq, tk), 1)
        causal = k_pos <= q_pos
        cq = cq_ref[0]
        hlane = lax.broadcasted_iota(jnp.int32, cq.shape, 1)
        for hh in range(2):
            h = 2 * p + hh
            qm = jnp.where(lo if hh == 0 else jnp.logical_not(lo), q2, 0.0)
            s = _mm_nt(qm, k2)
            cq_col = jnp.sum(jnp.where(hlane == h, cq, 0.0), axis=1, keepdims=True)
            s = s + (cq_col - ck_ref[0, pl.ds(h, 1), :])
            s = jnp.where(causal, s, NEG_INF)
            m_old = m_ref[hh]
            m_new = jnp.maximum(m_old, jnp.max(s, axis=1, keepdims=True))
            alpha = jnp.exp(m_old - m_new)
            pm = jnp.exp(s - m_new)
            l_ref[hh] = alpha * l_ref[hh] + jnp.sum(pm, axis=1, keepdims=True)
            acc_ref[hh] = alpha * acc_ref[hh] + jnp.dot(pm.astype(BF16), v2, preferred_element_type=F32)
            m_ref[hh] = m_new

    @pl.when(j == pl.num_programs(3) - 1)
    def _():
        o_ref[0] = jnp.where(lo, acc_ref[0] / l_ref[0], acc_ref[1] / l_ref[1])


def fox_attention(q_arr, q_off, k_arr, k_off, v_arr, v_off, cum_q, cum_row, past, T):
    B = q_arr.shape[0]
    S = cum_row.shape[2]
    tq = min(T, 512)
    tk = 512 if S % 512 == 0 else S
    nq, nk = T // tq, S // tk
    last = lambda i: (past + (i + 1) * tq - 1) // tk
    kj = lambda i, j: jnp.minimum(j, last(i))
    return pl.pallas_call(
        functools.partial(_fox_kernel, past=past, tq=tq, tk=tk),
        grid=(B, FOX_HEADS // 2, nq, nk),
        in_specs=[
            pl.BlockSpec((1, tq, LANES), lambda b, p, i, j: (b, i, q_off + p)),
            pl.BlockSpec((1, tk, LANES), lambda b, p, i, j: (b, kj(i, j), k_off + p)),
            pl.BlockSpec((1, tk, LANES), lambda b, p, i, j: (b, kj(i, j), v_off + p)),
            pl.BlockSpec((1, tq, FOX_HEADS), lambda b, p, i, j: (b, i, 0)),
            pl.BlockSpec((1, FOX_HEADS, tk), lambda b, p, i, j: (b, 0, kj(i, j))),
        ],
        out_specs=pl.BlockSpec((1, tq, LANES), lambda b, p, i, j: (b, i, p)),
        out_shape=jax.ShapeDtypeStruct((B, T, D), F32),
        scratch_shapes=[pltpu.VMEM((2, tq, 1), F32), pltpu.VMEM((2, tq, 1), F32),
                        pltpu.VMEM((2, tq, LANES), F32)],
        compiler_params=_cparams(("arbitrary", "arbitrary", "arbitrary", "arbitrary"), 48),
        name="fox_attention",
    )(q_arr, k_arr, v_arr, cum_q, cum_row)


def _trunk(x, mods, prm, wts, conv_st, ssd_st, hg_st, shift_st, rw_st, fk_past, fv_past, fl_past):
    B, T, _ = x.shape
    n = B * T
    sh1, sc1, gt1, sh2, sc2, gt2 = mods[0]
    proj = norm_inproj(x, prm['norm_mix_g'][0], sc1, sh1, wts['even_in'])
    ya, n_conv, n_ssd = ssd_mixer(proj, conv_st, ssd_st, prm['ssd_conv_w'][0], prm['ssd_conv_b'][0],
                                  prm['ssd_dt_bias'][0], prm['ssd_a_log'][0], prm['ssd_d'][0],
                                  prm['ssd_norm_g'][0])
    yb, n_hg = hgrn_mixer(proj, hg_st, wts['hgrn_lb'], prm['hgrn_norm_g'][0])
    x, hn = mix_outproj(ya, yb, wts['even_out'], x, gt1, prm['norm_ffn_g'][0], sc2, sh2)
    hn2 = hn.reshape(n, D)
    gates = moe_router(hn2, prm['moe_router_w'][0], prm['moe_router_bias'][0])
    ffn = moe_experts(hn2, gates, 0, prm['moe_w_gate'], prm['moe_w_up'], prm['moe_w_down'],
                      prm['moe_sh_gate'], prm['moe_sh_up'], prm['moe_sh_down'])
    x = residual(x, ffn, gt2, prm['final_norm_g'], False)

    sh1, sc1, gt1, sh2, sc2, gt2 = mods[1]
    proj = norm_inproj(x, prm['norm_mix_g'][1], sc1, sh1, wts['odd_in'])
    yc, n_shift, n_rw = rwkv_mixer(proj, shift_st, rw_st, prm['rwkv_mu'][0], prm['rwkv_w0'][0], prm['rwkv_w2'][0],
                                   prm['rwkv_a0'][0], prm['rwkv_a2'][0], prm['rwkv_g2'][0], prm['rwkv_k_k'][0],
                                   prm['rwkv_k_a'][0], prm['rwkv_r_k'][0], prm['rwkv_ln_g'][0],
                                   prm['rwkv_ln_b'][0])
    logf = fox_logf(proj, prm['fox_f_bias'][0])
    kf = proj[:, :, 4352:5376]
    vf = proj[:, :, 5376:6400]
    q_off = RW_COLS // LANES
    if fk_past is None:
        past = 0
        cum = time_cumsum(logf)
        k_arr, k_off, v_arr, v_off = proj, q_off + 8, proj, q_off + 16
    else:
        past = fk_past.shape[1]
        pad = (-(past + T)) % LANES
        lf_all = jnp.concatenate([fl_past, logf, jnp.zeros((B, pad, FOX_HEADS), F32)], axis=1)
        cum = time_cumsum(lf_all)
        k_arr = jnp.concatenate([fk_past.reshape(B, past, D), kf, jnp.zeros((B, pad, D), F32)], axis=1)
        v_arr = jnp.concatenate([fv_past.reshape(B, past, D), vf, jnp.zeros((B, pad, D), F32)], axis=1)
        k_off = v_off = 0
    cum_q = cum[:, past:past + T]
    yd = fox_attention(proj, q_off, k_arr, k_off, v_arr, v_off, cum_q, jnp.swapaxes(cum, 1, 2), past, T)
    x, hn = mix_outproj(yc, yd, wts['odd_out'], x, gt1, prm['norm_ffn_g'][1], sc2, sh2)
    hn2 = hn.reshape(n, D)
    gates = moe_router(hn2, prm['moe_router_w'][1], prm['moe_router_bias'][1])
    ffn = moe_experts(hn2, gates, 1, prm['moe_w_gate'], prm['moe_w_up'], prm['moe_w_down'],
                      prm['moe_sh_gate'], prm['moe_sh_up'], prm['moe_sh_down'])
    y = residual(x, ffn, gt2, prm['final_norm_g'], True)
    hs = (B, T, FOX_HEADS, FOX_HD)
    return (y, n_conv[None], n_ssd[None], n_hg[None], n_shift[None], n_rw[None],
            kf.reshape(hs)[None], vf.reshape(hs)[None], logf[None])


def _prep_weights(prm):
    ein = prm['even_in_w'][0]
    z, xbc, dt, hq, hf, hi, hgate = jnp.split(ein, [1024, 2304, 2320, 3344, 4368, 5392], axis=1)
    even_in = jnp.concatenate([z, hq, hf, hi, hgate, xbc, dt, jnp.zeros((D, LANES - SSD_HEADS), F32)], axis=1)
    odd_in = jnp.pad(prm['odd_in_w'][0], ((0, 0), (0, PROJ_COLS - prm['odd_in_w'].shape[2])))
    lb = jnp.cumsum(jax.nn.softmax(prm['hgrn_lb_logits'].astype(F32), axis=0), axis=0)[0]
    return {
        'even_in': even_in.astype(BF16),
        'odd_in': odd_in.astype(BF16),
        'even_out': prm['even_out_w'][0].astype(BF16),
        'odd_out': prm['odd_out_w'][0].astype(BF16),
        'hgrn_lb': lb,
    }


def kernel(x_prompt, x_sample, c_prompt, c_sample, state_ssd_conv, state_ssd, state_hgrn, state_rwkv_shift, state_rwkv, cache_fox_k, cache_fox_v, cache_fox_logf, ada_w, ada_b, norm_mix_g, norm_ffn_g, final_norm_g, even_in_w, even_out_w, ssd_conv_w, ssd_conv_b, ssd_dt_bias, ssd_a_log, ssd_d, ssd_norm_g, hgrn_lb_logits, hgrn_norm_g, odd_in_w, odd_out_w, rwkv_mu, rwkv_w0, rwkv_w2, rwkv_a0, rwkv_a2, rwkv_g2, rwkv_k_k, rwkv_k_a, rwkv_r_k, rwkv_ln_g, rwkv_ln_b, fox_f_bias, moe_router_w, moe_router_bias, moe_w_gate, moe_w_up, moe_w_down, moe_sh_gate, moe_sh_up, moe_sh_down):
    prm = dict(ada_w=ada_w, ada_b=ada_b, norm_mix_g=norm_mix_g, norm_ffn_g=norm_ffn_g, final_norm_g=final_norm_g,
               even_in_w=even_in_w, even_out_w=even_out_w, ssd_conv_w=ssd_conv_w, ssd_conv_b=ssd_conv_b,
               ssd_dt_bias=ssd_dt_bias, ssd_a_log=ssd_a_log, ssd_d=ssd_d, ssd_norm_g=ssd_norm_g,
               hgrn_lb_logits=hgrn_lb_logits, hgrn_norm_g=hgrn_norm_g,
               odd_in_w=odd_in_w, odd_out_w=odd_out_w, rwkv_mu=rwkv_mu, rwkv_w0=rwkv_w0, rwkv_w2=rwkv_w2,
               rwkv_a0=rwkv_a0, rwkv_a2=rwkv_a2, rwkv_g2=rwkv_g2, rwkv_k_k=rwkv_k_k, rwkv_k_a=rwkv_k_a,
               rwkv_r_k=rwkv_r_k, rwkv_ln_g=rwkv_ln_g, rwkv_ln_b=rwkv_ln_b, fox_f_bias=fox_f_bias,
               moe_router_w=moe_router_w, moe_router_bias=moe_router_bias, moe_w_gate=moe_w_gate,
               moe_w_up=moe_w_up, moe_w_down=moe_w_down, moe_sh_gate=moe_sh_gate, moe_sh_up=moe_sh_up,
               moe_sh_down=moe_sh_down)
    wts = _prep_weights(prm)
    bp, bs = x_prompt.shape[0], x_sample.shape[0]
    depth = ada_w.shape[0]
    c_all = jnp.concatenate([c_prompt, c_sample], axis=0)
    rows = c_all.shape[0]
    c_all = jnp.pad(c_all, ((0, (-rows) % SUBLANES), (0, 0)))
    mod = ada_mod(c_all, ada_w, ada_b)

    def mods_for(lo, hi):
        return [tuple(m[:, None, :] for m in jnp.split(mod[l, lo:hi], 6, axis=-1)) for l in range(depth)]

    z = lambda *s: jnp.zeros(s, F32)
    out_p = _trunk(x_prompt, mods_for(0, bp), prm, wts,
                   z(bp, SSD_CONV - 1, SSD_CONV_DIM), z(bp, SSD_HEADS, SSD_HD, SSD_STATE),
                   z(bp, HG_HEADS, HG_DK, HG_DV), z(bp, RW_COLS), z(bp, RW_HEADS, RW_N, RW_N),
                   None, None, None)
    out_s = _trunk(x_sample, mods_for(bp, bp + bs), prm, wts,
                   state_ssd_conv[0], state_ssd[0], state_hgrn[0], state_rwkv_shift[0], state_rwkv[0],
                   cache_fox_k[0], cache_fox_v[0], cache_fox_logf[0])
    return (out_p[0], out_s[0]) + tuple(out_p[1:]) + tuple(out_s[1:])
```

```python
import functools

import jax
import jax.numpy as jnp
from jax import lax
from jax.experimental import pallas as pl
from jax.experimental.pallas import tpu as pltpu

F32 = jnp.float32
BF16 = jnp.bfloat16
HI = lax.Precision.HIGHEST

D = 1024
EPS = 1e-6
LANES = 128
SUBLANES = 8

SSD_HEADS = 16
SSD_HD = 64
SSD_STATE = 64
SSD_CONV = 4
SSD_CONV_DIM = 1280
SSD_L = 64
HG_HEADS = 8
HG_DK = 128
HG_DV = 128
HG_L = 16
RW_HEADS = 16
RW_N = 64
RW_COLS = 3328
RW_LN_EPS = 64e-5
RW_L = 64
FOX_HEADS = 16
FOX_HD = 64
FOX_AW = FOX_HEADS * LANES
LOG2E = 1.4426950408889634
ODD_Q_BLOCK, ODD_K_BLOCK, ODD_V_BLOCK = 3, 4, 5
ODD_LORA_BLOCK = 24
ODD_FL_BLOCK = 50
RW_LORA = 256
N_EXPERTS = 64
TOP_K = 6
N_GROUPS = 8
TOPK_GROUPS = 4
D_EXPERT = 256
ROUTED_SCALE = 2.5
PROJ_COLS = 6528
PROJ_TN = 2176

NEG_INF = float("-inf")


def _cparams(sem, vmem_mb=None):
    kw = dict(dimension_semantics=sem)
    if vmem_mb is not None:
        kw["vmem_limit_bytes"] = vmem_mb * 1024 * 1024
    return pltpu.CompilerParams(**kw)


def _mm(a, b):
    return jnp.dot(a.astype(BF16), b.astype(BF16), preferred_element_type=F32)


def _mm_nt(a, b):
    return lax.dot_general(a.astype(BF16), b.astype(BF16), (((1,), (1,)), ((), ())),
                           preferred_element_type=F32)


def _mm_hi(a, b):
    return jnp.dot(a, b, precision=HI, preferred_element_type=F32)


def _sigmoid(x):
    return 1.0 / (1.0 + jnp.exp(-x))


def _silu(x):
    return x * _sigmoid(x)


def _softplus(x):
    return jnp.maximum(x, 0.0) + jnp.log1p(jnp.exp(-jnp.abs(x)))


def _row_tiles(G, gs, target):
    if gs >= target:
        assert gs % target == 0
        return 1, target
    gb = max(1, min(G, target // gs))
    while G % gb:
        gb -= 1
    return gb, gs


def _ada_kernel(c_ref, w_ref, b_ref, o_ref):
    c = c_ref[...]
    o_ref[0] = _mm(_silu(c), w_ref[0]) + b_ref[0]


def ada_mod(c_all, ada_w, ada_b):
    depth = ada_w.shape[0]
    bp = c_all.shape[0]
    tn = 1024
    return pl.pallas_call(
        _ada_kernel,
        grid=(depth, 6 * D // tn),
        in_specs=[
            pl.BlockSpec((bp, D), lambda l, j: (0, 0)),
            pl.BlockSpec((1, D, tn), lambda l, j: (l, 0, j)),
            pl.BlockSpec((1, 1, tn), lambda l, j: (l, 0, j)),
        ],
        out_specs=pl.BlockSpec((1, bp, tn), lambda l, j: (l, 0, j)),
        out_shape=jax.ShapeDtypeStruct((depth, bp, 6 * D), F32),
        compiler_params=_cparams(("arbitrary", "arbitrary")),
        name="ada_mod",
    )(c_all, ada_w, ada_b.reshape(depth, 1, 6 * D))


def _rms_mod(x, g, sc, sh):
    y = x * lax.rsqrt(jnp.mean(x * x, axis=-1, keepdims=True) + EPS)
    return (y * g) * (1.0 + sc) + sh


def _inproj_kernel(x_ref, g_ref, sc_ref, sh_ref, w_ref, o_ref, xn_ref):
    @pl.when(pl.program_id(1) == 0)
    def _():
        hn = _rms_mod(x_ref[...], g_ref[...], sc_ref[...], sh_ref[...])
        xn_ref[...] = hn.reshape(xn_ref.shape).astype(BF16)

    o_ref[...] = jnp.dot(xn_ref[...], w_ref[...], preferred_element_type=F32)


def norm_inproj(x, g, sc, sh, w_bf):
    G, gs, _ = x.shape
    gb, tb = _row_tiles(G, gs, 512)
    nt = gs // tb
    tt = gb * tb
    n = G * gs
    out = pl.pallas_call(
        _inproj_kernel,
        grid=(n // tt, PROJ_COLS // PROJ_TN),
        in_specs=[
            pl.BlockSpec((gb, tb, D), lambda i, j: (i // nt, i % nt, 0)),
            pl.BlockSpec((1, D), lambda i, j: (0, 0)),
            pl.BlockSpec((gb, 1, D), lambda i, j: (i // nt, 0, 0)),
            pl.BlockSpec((gb, 1, D), lambda i, j: (i // nt, 0, 0)),
            pl.BlockSpec((D, PROJ_TN), lambda i, j: (0, j)),
        ],
        out_specs=pl.BlockSpec((tt, PROJ_TN), lambda i, j: (i, j)),
        out_shape=jax.ShapeDtypeStruct((n, PROJ_COLS), F32),
        scratch_shapes=[pltpu.VMEM((tt, D), BF16)],
        compiler_params=_cparams(("arbitrary", "arbitrary"), 48),
        name="norm_inproj",
    )(x, g.reshape(1, D), sc, sh, w_bf)
    return out.reshape(G, gs, PROJ_COLS)


def _outproj_kernel(ya_ref, yb_ref, w_ref, x_ref, gt_ref, g_ref, sc_ref, sh_ref, xo_ref, hn_ref):
    gb, tb, _ = x_ref.shape
    out = (jnp.dot(ya_ref[...].astype(BF16), w_ref[0:D, :], preferred_element_type=F32)
           + jnp.dot(yb_ref[...].astype(BF16), w_ref[D:2 * D, :], preferred_element_type=F32))
    x = x_ref[...] + gt_ref[...] * out.reshape(gb, tb, D)
    xo_ref[...] = x
    hn_ref[...] = _rms_mod(x, g_ref[...], sc_ref[...], sh_ref[...])


def mix_outproj(ya, yb, w_bf, x, gt, g, sc, sh):
    G, gs, _ = x.shape
    gb, tb = _row_tiles(G, gs, 512)
    nt = gs // tb
    tt = gb * tb
    n = G * gs
    row3 = pl.BlockSpec((gb, tb, D), lambda i: (i // nt, i % nt, 0))
    mod3 = pl.BlockSpec((gb, 1, D), lambda i: (i // nt, 0, 0))
    return pl.pallas_call(
        _outproj_kernel,
        grid=(n // tt,),
        in_specs=[
            pl.BlockSpec((tt, D), lambda i: (i, 0)),
            pl.BlockSpec((tt, D), lambda i: (i, 0)),
            pl.BlockSpec((2 * D, D), lambda i: (0, 0)),
            row3, mod3,
            pl.BlockSpec((1, D), lambda i: (0, 0)),
            mod3, mod3,
        ],
        out_specs=[row3, row3],
        out_shape=[jax.ShapeDtypeStruct((G, gs, D), F32), jax.ShapeDtypeStruct((G, gs, D), F32)],
        compiler_params=_cparams(("arbitrary",), 48),
        name="mix_outproj",
    )(ya.reshape(n, D), yb.reshape(n, D), w_bf, x, gt, g.reshape(1, D), sc, sh)


def _ssd_kernel(z_ref, xbc_ref, dt_ref, conv0_ref, st0_ref, cw_ref, cb_ref, dtb_ref, alog_ref, dfull_ref,
                ng_ref, y_ref, convo_ref, sto_ref, ext_ref, st_ref, yacc_ref):
    L = SSD_L
    c = pl.program_id(1)

    @pl.when(c == 0)
    def _():
        ext_ref[0:SUBLANES, :] = conv0_ref[0]
        st_ref[...] = st0_ref[0]

    u = xbc_ref[0]
    ext_ref[SUBLANES:SUBLANES + L, :] = u
    acc = cb_ref[...] + u * cw_ref[SSD_CONV - 1:SSD_CONV, :]
    for i in range(SSD_CONV - 1):
        sh = SSD_CONV - 1 - i
        acc = acc + ext_ref[SUBLANES - sh:SUBLANES - sh + L, :] * cw_ref[i:i + 1, :]
    ext_ref[0:SUBLANES, :] = ext_ref[L:L + SUBLANES, :]
    xbc = _silu(acc)
    xs = xbc[:, 0:D]
    bm2 = xbc[:, D:D + LANES]
    cm2 = xbc[:, D + LANES:D + 2 * LANES]

    dt = _softplus(dt_ref[0] + dtb_ref[...])
    a = dt * (-jnp.exp(alog_ref[...]))
    row = lax.broadcasted_iota(jnp.int32, (L, L), 0)
    col = lax.broadcasted_iota(jnp.int32, (L, L), 1)
    tril = (col <= row).astype(F32)
    triu = (row <= col).astype(F32)
    cum = _mm_hi(tril, a)
    cum_t = _mm_hi(a.T, triu)
    eh = lax.broadcasted_iota(jnp.int32, (LANES, D), 0)
    ec = lax.broadcasted_iota(jnp.int32, (LANES, D), 1)
    expand = (ec // SSD_HD == eh).astype(F32)
    dt_full = _mm_hi(dt, expand)
    cum_full = _mm_hi(cum, expand)
    ecum_full = jnp.exp(cum_full)
    last_full = cum_full[L - 1:L, :]
    w_full = jnp.exp(last_full - cum_full)
    xdt = xs * dt_full

    lane = lax.broadcasted_iota(jnp.int32, (L, LANES), 1)
    lo = lane < SSD_HD
    tri2 = (lane % L) <= lax.broadcasted_iota(jnp.int32, (L, LANES), 0)
    bd_r = lax.broadcasted_iota(jnp.int32, (LANES, LANES), 0) // SSD_HD
    bd_c = lax.broadcasted_iota(jnp.int32, (LANES, LANES), 1) // SSD_HD
    blockdiag = bd_r == bd_c
    bm_r = pltpu.roll(bm2, SSD_HD, 1)
    cm_r = pltpu.roll(cm2, SSD_HD, 1)
    bmstack = jnp.concatenate([bm2, bm2], axis=0)

    for g in range(2):
        grp_lo = lo if g == 0 else jnp.logical_not(lo)
        cm_m = jnp.where(grp_lo, cm2, 0.0)
        cb_dup = _mm_nt(cm_m, bmstack)
        cm_dup = jnp.where(grp_lo, cm2, cm_r)
        bm_dup = jnp.where(grp_lo, bm2, bm_r)
        for pp in range(4):
            p = 4 * g + pp
            sl = slice(LANES * p, LANES * (p + 1))
            cum_pair = cum_full[:, sl]
            cum_row = jnp.concatenate([cum_t[2 * p:2 * p + 1, :], cum_t[2 * p + 1:2 * p + 2, :]], axis=1)
            s2 = cb_dup * jnp.exp(jnp.where(tri2, cum_pair - cum_row, NEG_INF))
            x2 = xdt[:, sl]
            xstack = jnp.concatenate([jnp.where(lo, x2, 0.0), jnp.where(lo, 0.0, x2)], axis=0)
            hp = st_ref[p]
            lhs = jnp.concatenate([s2, cm_dup * ecum_full[:, sl]], axis=1)
            rhs = jnp.concatenate([xstack, hp], axis=0)
            yacc_ref[:, sl] = _mm(lhs, rhs)
            bw_t = (bm_dup * w_full[:, sl]).T
            upd = _mm(bw_t, x2)
            st_ref[p] = jnp.where(blockdiag, hp * ecum_full[L - 1:L, sl] + upd, 0.0)

    y = yacc_ref[...] + xs * dfull_ref[...]
    z = z_ref[0]
    yz = y * _silu(z)
    y_ref[0] = yz * lax.rsqrt(jnp.mean(yz * yz, axis=-1, keepdims=True) + EPS) * ng_ref[...]

    @pl.when(c == pl.num_programs(1) - 1)
    def _():
        convo_ref[0] = ext_ref[0:SUBLANES, :]
        sto_ref[0] = st_ref[...]


def ssd_mixer(proj, conv_state, ssd_state, conv_w, conv_b, dt_bias, a_log, d_skip, norm_g):
    B, T, _ = proj.shape
    L = SSD_L
    assert T % L == 0
    H, P, N = SSD_HEADS, SSD_HD, SSD_STATE
    conv0 = jnp.pad(conv_state, ((0, 0), (SUBLANES - (SSD_CONV - 1), 0), (0, 0)))
    st_t = jnp.swapaxes(ssd_state, -1, -2).reshape(B, H // 2, 2, N, P)
    st0 = jnp.zeros((B, H // 2, 2, N, 2, P), F32)
    st0 = st0.at[:, :, 0, :, 0, :].set(st_t[:, :, 0]).at[:, :, 1, :, 1, :].set(st_t[:, :, 1])
    st0 = st0.reshape(B, H // 2, 2 * N, 2 * P)
    cw = jnp.pad(conv_w, ((0, SUBLANES - SSD_CONV), (0, 0)))
    pad16 = lambda v: jnp.pad(v.reshape(1, H), ((0, 0), (0, LANES - H)))
    dfull = jnp.repeat(d_skip, P).reshape(1, D)
    vec = lambda w: pl.BlockSpec((1, w), lambda b, c: (0, 0))
    y, convo, sto = pl.pallas_call(
        _ssd_kernel,
        grid=(B, T // L),
        in_specs=[
            pl.BlockSpec((1, L, D), lambda b, c: (b, c, 0)),
            pl.BlockSpec((1, L, SSD_CONV_DIM), lambda b, c: (b, c, 4)),
            pl.BlockSpec((1, L, LANES), lambda b, c: (b, c, 50)),
            pl.BlockSpec((1, SUBLANES, SSD_CONV_DIM), lambda b, c: (b, 0, 0)),
            pl.BlockSpec((1, H // 2, LANES, LANES), lambda b, c: (b, 0, 0, 0)),
            pl.BlockSpec((SUBLANES, SSD_CONV_DIM), lambda b, c: (0, 0)),
            vec(SSD_CONV_DIM), vec(LANES), vec(LANES), vec(D), vec(D),
        ],
        out_specs=[
            pl.BlockSpec((1, L, D), lambda b, c: (b, c, 0)),
            pl.BlockSpec((1, SUBLANES, SSD_CONV_DIM), lambda b, c: (b, 0, 0)),
            pl.BlockSpec((1, H // 2, LANES, LANES), lambda b, c: (b, 0, 0, 0)),
        ],
        out_shape=[
            jax.ShapeDtypeStruct((B, T, D), F32),
            jax.ShapeDtypeStruct((B, SUBLANES, SSD_CONV_DIM), F32),
            jax.ShapeDtypeStruct((B, H // 2, LANES, LANES), F32),
        ],
        scratch_shapes=[
            pltpu.VMEM((SUBLANES + L, SSD_CONV_DIM), F32),
            pltpu.VMEM((H // 2, LANES, LANES), F32),
            pltpu.VMEM((L, D), F32),
        ],
        compiler_params=_cparams(("arbitrary", "arbitrary"), 48),
        name="ssd_mixer",
    )(proj, proj, proj, conv0, st0, cw, conv_b.reshape(1, SSD_CONV_DIM), pad16(dt_bias), pad16(a_log),
      dfull, norm_g.reshape(1, D))
    sto = sto.reshape(B, H // 2, 2, N, 2, P)
    new_st = jnp.stack([sto[:, :, 0, :, 0, :], sto[:, :, 1, :, 1, :]], axis=2).reshape(B, H, N, P)
    return y, convo[:, SUBLANES - (SSD_CONV - 1):], jnp.swapaxes(new_st, -1, -2)


def _hgrn_kernel(q_ref, f_ref, i_ref, gate_ref, st0_ref, lb_ref, ng_ref, y_ref, sto_ref, st_ref):
    L = HG_L
    c = pl.program_id(1)

    @pl.when(c == 0)
    def _():
        st_ref[...] = st0_ref[0]

    lb = lb_ref[...]
    f = lb + (1.0 - lb) * _sigmoid(f_ref[0])
    logf = jnp.log(f)
    q = _silu(q_ref[0]) * (HG_DK ** -0.5)
    k = 1.0 - f
    v = i_ref[0]
    row = lax.broadcasted_iota(jnp.int32, (L, L), 0)
    col = lax.broadcasted_iota(jnp.int32, (L, L), 1)
    cum = _mm_hi((col <= row).astype(F32), logf)
    qe = q * jnp.exp(cum)
    last = cum[L - 1:L, :]
    kw = k * jnp.exp(last - cum)
    elast = jnp.exp(last)
    srow = lax.broadcasted_iota(jnp.int32, (L, HG_DK), 0)
    for h in range(HG_HEADS):
        sl = slice(HG_DK * h, HG_DK * (h + 1))
        gh, qh, kh, vh = cum[:, sl], q[:, sl], k[:, sl], v[:, sl]
        st = st_ref[h]
        inter = _mm_nt(qe[:, sl], st)
        rows = []
        for t in range(L):
            dec = jnp.exp(jnp.where(srow <= t, gh[t:t + 1, :] - gh, NEG_INF))
            sc = jnp.sum(qh[t:t + 1, :] * kh * dec, axis=1, keepdims=True)
            rows.append(jnp.sum(sc * vh, axis=0, keepdims=True))
        o = inter + jnp.concatenate(rows, axis=0)
        st_ref[h] = st * elast[:, sl] + _mm(vh.T, kw[:, sl])
        on = o * lax.rsqrt(jnp.mean(o * o, axis=-1, keepdims=True) + EPS) * ng_ref[:, sl]
        y_ref[0, :, sl] = on * _silu(gate_ref[0, :, sl])

    @pl.when(c == pl.num_programs(1) - 1)
    def _():
        sto_ref[0] = st_ref[...]


def hgrn_mixer(proj, hg_state, lb, norm_g):
    B, T, _ = proj.shape
    L = HG_L
    assert T % L == 0
    st0 = jnp.swapaxes(hg_state, -1, -2)
    col = lambda j: pl.BlockSpec((1, L, D), lambda b, c: (b, c, j))
    st_spec = pl.BlockSpec((1, HG_HEADS, HG_DV, HG_DK), lambda b, c: (b, 0, 0, 0))
    vec = pl.BlockSpec((1, D), lambda b, c: (0, 0))
    y, sto = pl.pallas_call(
        _hgrn_kernel,
        grid=(B, T // L),
        in_specs=[col(1), col(2), col(3), col(4), st_spec, vec, vec],
        out_specs=[pl.BlockSpec((1, L, D), lambda b, c: (b, c, 0)), st_spec],
        out_shape=[jax.ShapeDtypeStruct((B, T, D), F32),
                   jax.ShapeDtypeStruct((B, HG_HEADS, HG_DV, HG_DK), F32)],
        scratch_shapes=[pltpu.VMEM((HG_HEADS, HG_DV, HG_DK), F32)],
        compiler_params=_cparams(("arbitrary", "arbitrary")),
        name="hgrn_mixer",
    )(proj, proj, proj, proj, st0, lb.reshape(1, D), norm_g.reshape(1, D))
    return y, jnp.swapaxes(sto, -1, -2)


def _first_max_onehot(vals, iota, axis, size):
    m = jnp.max(vals, axis=axis, keepdims=True)
    idx = jnp.min(jnp.where(vals == m, iota, size), axis=axis, keepdims=True)
    return iota == idx, m


def _router_kernel(x_ref, wt_ref, bias_ref, o_ref):
    tt = x_ref.shape[0]
    per = N_EXPERTS // N_GROUPS
    logits = lax.dot_general(wt_ref[...], x_ref[...], (((1,), (1,)), ((), ())),
                             precision=HI, preferred_element_type=F32)
    scores = _sigmoid(logits)
    biased = scores + bias_ref[...]
    b3 = biased.reshape(N_GROUPS, per, tt)
    i3 = lax.broadcasted_iota(jnp.int32, (N_GROUPS, per, tt), 1)
    oh, m1 = _first_max_onehot(b3, i3, 1, per)
    m2 = jnp.max(jnp.where(oh, NEG_INF, b3), axis=1, keepdims=True)
    grp = (m1 + m2).reshape(N_GROUPS, tt)
    ig = lax.broadcasted_iota(jnp.int32, (N_GROUPS, tt), 0)
    gsel = jnp.zeros((N_GROUPS, tt), jnp.bool_)
    for _ in range(TOPK_GROUPS):
        oh, _m = _first_max_onehot(grp, ig, 0, N_GROUPS)
        gsel = jnp.logical_or(gsel, oh)
        grp = jnp.where(oh, NEG_INF, grp)
    gmask = jnp.where(gsel, 1.0, 0.0).reshape(N_GROUPS, 1, tt)
    masked = jnp.where(gmask > 0.0, b3, NEG_INF).reshape(N_EXPERTS, tt)
    ie = lax.broadcasted_iota(jnp.int32, (N_EXPERTS, tt), 0)
    esel = jnp.zeros((N_EXPERTS, tt), F32)
    for _ in range(TOP_K):
        oh, _m = _first_max_onehot(masked, ie, 0, N_EXPERTS)
        esel = jnp.where(oh, 1.0, esel)
        masked = jnp.where(oh, NEG_INF, masked)
    wsel = scores * esel
    gates = wsel / jnp.sum(wsel, axis=0, keepdims=True) * ROUTED_SCALE
    o_ref[...] = gates.T


def moe_router(hn2d, router_w, router_bias):
    n = hn2d.shape[0]
    tt = min(n, 512)
    return pl.pallas_call(
        _router_kernel,
        grid=(n // tt,),
        in_specs=[
            pl.BlockSpec((tt, D), lambda i: (i, 0)),
            pl.BlockSpec((N_EXPERTS, D), lambda i: (0, 0)),
            pl.BlockSpec((N_EXPERTS, 1), lambda i: (0, 0)),
        ],
        out_specs=pl.BlockSpec((tt, N_EXPERTS), lambda i: (i, 0)),
        out_shape=jax.ShapeDtypeStruct((n, N_EXPERTS), F32),
        compiler_params=_cparams(("arbitrary",)),
        name="moe_router",
    )(hn2d, router_w.T, router_bias.reshape(N_EXPERTS, 1))


def _ffn(xb, wg, wu, wd, gate_col):
    h = _silu(jnp.dot(xb, wg.astype(BF16), preferred_element_type=F32)) * jnp.dot(
        xb, wu.astype(BF16), preferred_element_type=F32)
    if gate_col is not None:
        h = h * gate_col
    return jnp.dot(h.astype(BF16), wd.astype(BF16), preferred_element_type=F32)


def _moe_kernel(x_ref, gates_ref, wg_ref, wu_ref, wd_ref, sg_ref, su_ref, sd_ref, o_ref, xb_ref, acc_ref):
    e = pl.program_id(1)

    @pl.when(e == 0)
    def _():
        xb_ref[...] = x_ref[...].astype(BF16)
        acc_ref[...] = _ffn(xb_ref[...], sg_ref[0], su_ref[0], sd_ref[0], None)

    gates = gates_ref[...]
    lane = lax.broadcasted_iota(jnp.int32, gates.shape, 1)
    gate_col = jnp.sum(jnp.where(lane == e, gates, 0.0), axis=1, keepdims=True)
    acc_ref[...] += _ffn(xb_ref[...], wg_ref[0, 0], wu_ref[0, 0], wd_ref[0, 0], gate_col)

    @pl.when(e == pl.num_programs(1) - 1)
    def _():
        o_ref[...] = acc_ref[...]


def moe_experts(hn2d, gates, l, w_gate, w_up, w_down, sh_gate, sh_up, sh_down):
    n = hn2d.shape[0]
    tt = min(n, 1024)
    return pl.pallas_call(
        _moe_kernel,
        grid=(n // tt, N_EXPERTS),
        in_specs=[
            pl.BlockSpec((tt, D), lambda i, e: (i, 0)),
            pl.BlockSpec((tt, N_EXPERTS), lambda i, e: (i, 0)),
            pl.BlockSpec((1, 1, D, D_EXPERT), lambda i, e: (l, e, 0, 0)),
            pl.BlockSpec((1, 1, D, D_EXPERT), lambda i, e: (l, e, 0, 0)),
            pl.BlockSpec((1, 1, D_EXPERT, D), lambda i, e: (l, e, 0, 0)),
            pl.BlockSpec((1, D, D_EXPERT), lambda i, e: (l, 0, 0)),
            pl.BlockSpec((1, D, D_EXPERT), lambda i, e: (l, 0, 0)),
            pl.BlockSpec((1, D_EXPERT, D), lambda i, e: (l, 0, 0)),
        ],
        out_specs=pl.BlockSpec((tt, D), lambda i, e: (i, 0)),
        out_shape=jax.ShapeDtypeStruct((n, D), F32),
        scratch_shapes=[pltpu.VMEM((tt, D), BF16), pltpu.VMEM((tt, D), F32)],
        compiler_params=_cparams(("arbitrary", "arbitrary"), 56),
        name="moe_experts",
    )(hn2d, gates, w_gate, w_up, w_down, sh_gate, sh_up, sh_down)


def _residual_kernel(x_ref, f_ref, gt_ref, g_ref, o_ref, *, final_norm):
    x = x_ref[...] + gt_ref[...] * f_ref[...]
    if final_norm:
        x = x * lax.rsqrt(jnp.mean(x * x, axis=-1, keepdims=True) + EPS) * g_ref[...]
    o_ref[...] = x


def residual(x, ffn, gt, g, final_norm):
    G, gs, _ = x.shape
    gb, tb = _row_tiles(G, gs, 512)
    nt = gs // tb
    row3 = pl.BlockSpec((gb, tb, D), lambda i: (i // nt, i % nt, 0))
    return pl.pallas_call(
        functools.partial(_residual_kernel, final_norm=final_norm),
        grid=(G * gs // (gb * tb),),
        in_specs=[row3, row3, pl.BlockSpec((gb, 1, D), lambda i: (i // nt, 0, 0)),
                  pl.BlockSpec((1, D), lambda i: (0, 0))],
        out_specs=row3,
        out_shape=jax.ShapeDtypeStruct((G, gs, D), F32),
        compiler_params=_cparams(("arbitrary",)),
        name="residual",
    )(x, ffn.reshape(G, gs, D), gt, g.reshape(1, D))


def _head_sum(x):
    L = x.shape[0]
    lo = lax.broadcasted_iota(jnp.int32, (L, LANES), 1) < RW_N
    outs = []
    for p in range(RW_HEADS // 2):
        x2 = x[:, LANES * p:LANES * (p + 1)]
        s_lo = jnp.sum(jnp.where(lo, x2, 0.0), axis=1, keepdims=True)
        s_hi = jnp.sum(jnp.where(lo, 0.0, x2), axis=1, keepdims=True)
        outs.append(jnp.where(lo, s_lo, s_hi))
    return jnp.concatenate(outs, axis=1)


def _rwkv_kernel(rkv_ref, lora_ref, shift0_ref, st0_ref, mu_ref, w0_ref, w2_ref, a0_ref, a2_ref, g2_ref, kk_ref, ka_ref,
                 rk_ref, lng_ref, lnb_ref, y_ref, shifto_ref, sto_ref,
                 ext_ref, st_ref, r_s, w_s, k_s, b_s, nkk_s, vt_s, ot_s, o_s):
    L = RW_L
    N = RW_N
    NP = RW_HEADS // 2
    c = pl.program_id(1)

    @pl.when(c == 0)
    def _():
        ext_ref[0:SUBLANES, :] = shift0_ref[0]
        st_ref[...] = st0_ref[0]

    pr = jnp.concatenate([rkv_ref[0], lora_ref[0]], axis=1)
    ext_ref[SUBLANES:SUBLANES + L, :] = pr
    prev = ext_ref[SUBLANES - 1:SUBLANES - 1 + L, :]
    ext_ref[0:SUBLANES, :] = ext_ref[L:L + SUBLANES, :]
    ps = pr + (prev - pr) * mu_ref[...]
    r = ps[:, 0:D]
    k = ps[:, D:2 * D]
    v = ps[:, 2 * D:3 * D]
    wa = ps[:, 3 * D:3 * D + LANES]
    gd = ps[:, 3 * D + LANES:3 * D + 2 * LANES]
    w = -_softplus(-(w0_ref[...] + _mm(jnp.tanh(wa), w2_ref[...]))) - 0.5
    wdec = jnp.exp(-jnp.exp(w))
    a = _sigmoid(a0_ref[...] + _mm(wa, a2_ref[...]))
    g = _mm(_sigmoid(gd), g2_ref[...])
    kk = k * kk_ref[...]
    kk = kk * lax.rsqrt(jnp.maximum(_head_sum(kk * kk), 1e-24))
    k = k * (1.0 + (a - 1.0) * ka_ref[...])
    bonus = _head_sum(r * k * rk_ref[...]) * v
    b = kk * a
    for p in range(NP):
        ps_ = slice(LANES * p, LANES * (p + 1))
        r_s[p] = r[:, ps_]
        w_s[p] = wdec[:, ps_]
        k_s[p] = k[:, ps_]
        b_s[p] = b[:, ps_]
        nkk_s[p] = -kk[:, ps_]
        v2 = v[:, ps_]
        vt_s[p] = jnp.concatenate([v2[:, 0:N], v2[:, N:2 * N]], axis=0).T
    ot_s[...] = jnp.zeros(ot_s.shape, F32)

    bd_r = lax.broadcasted_iota(jnp.int32, (2 * LANES, 2 * LANES), 0) // N
    bd_c = lax.broadcasted_iota(jnp.int32, (2 * LANES, 2 * LANES), 1) // N
    seg_ones = jnp.where(bd_r == bd_c, 1.0, 0.0).astype(BF16)

    def seg_sum(parts):
        half = len(parts) // 2
        outs = []
        for grp in (parts[:half], parts[half:]):
            x = jnp.concatenate([jnp.concatenate(grp[0::2], axis=0), jnp.concatenate(grp[1::2], axis=0)], axis=1)
            y = jnp.dot(x.astype(BF16), seg_ones, preferred_element_type=F32)
            for i in range(len(grp)):
                outs.append(y[N * (i // 2):N * (i // 2 + 1), LANES * (i % 2):LANES * (i % 2 + 1)])
        return outs

    lane_time = lax.broadcasted_iota(jnp.int32, (N, LANES), 1) % L
    row = lambda ref, p, t: ref[p, pl.ds(t, 1), :]

    def step(t, carry):
        tp = jnp.maximum(t - 1, 0)
        parts = []
        for p in range(NP):
            s = st_ref[p]
            parts += [s * row(nkk_s, p, t), jnp.where(lane_time == t, vt_s[p], 0.0), s * row(r_s, p, tp)]
        res = seg_sum(parts)
        for p in range(NP):
            sa, vcol, o_prev = res[3 * p:3 * p + 3]
            st_ref[p] = st_ref[p] * row(w_s, p, t) + sa * row(b_s, p, t) + vcol * row(k_s, p, t)
            ot_s[p] = jnp.where(lane_time == t - 1, o_prev, ot_s[p])
        return carry

    lax.fori_loop(0, L, step, 0, unroll=2)
    o_last = seg_sum([st_ref[p] * row(r_s, p, L - 1) for p in range(NP)])
    for p in range(NP):
        ot_s[p] = jnp.where(lane_time == L - 1, o_last[p], ot_s[p])

    for p in range(NP):
        ot = ot_s[p]
        mean = jnp.mean(ot, axis=0, keepdims=True)
        cen = ot - mean
        var = jnp.mean(cen * cen, axis=0, keepdims=True)
        on_t = (cen * lax.rsqrt(var + RW_LN_EPS)).T
        o_s[:, LANES * p:LANES * p + N] = on_t[0:L]
        o_s[:, LANES * p + N:LANES * (p + 1)] = on_t[L:2 * L]
    o = o_s[...] * lng_ref[...] + lnb_ref[...]
    y_ref[0] = (o + bonus) * g

    @pl.when(c == pl.num_programs(1) - 1)
    def _():
        shifto_ref[0] = ext_ref[0:SUBLANES, :]
        sto_ref[0] = st_ref[...]


def rwkv_mixer(proj, shift_state, rw_state, mu, w0, w2, a0, a2, g2, k_k, k_a, r_k, ln_g, ln_b):
    B, T, _ = proj.shape
    L = RW_L
    assert T % L == 0
    H, N = RW_HEADS, RW_N
    shift0 = jnp.pad(shift_state[:, None, :], ((0, 0), (SUBLANES - 1, 0), (0, 0)))
    w2p = jnp.pad(w2, ((0, LANES - w2.shape[0]), (0, 0)))
    a2p = jnp.pad(a2, ((LANES - a2.shape[0], 0), (0, 0)))
    vec = lambda w: pl.BlockSpec((1, w), lambda b, c: (0, 0))
    mat = pl.BlockSpec((LANES, D), lambda b, c: (0, 0))
    st_spec = pl.BlockSpec((1, H // 2, N, 2 * N), lambda b, c: (b, 0, 0, 0))
    sh_spec = pl.BlockSpec((1, SUBLANES, RW_COLS), lambda b, c: (b, 0, 0))
    rows = pltpu.VMEM((H // 2, L, 2 * N), F32)
    cols = pltpu.VMEM((H // 2, N, 2 * L), F32)
    st0 = jnp.swapaxes(rw_state.reshape(B, H // 2, 2, N, N), 2, 3).reshape(B, H // 2, N, 2 * N)
    y, shifto, sto = pl.pallas_call(
        _rwkv_kernel,
        grid=(B, T // L),
        in_specs=[pl.BlockSpec((1, L, 3 * D), lambda b, c: (b, c, 0)),
                  pl.BlockSpec((1, L, RW_LORA), lambda b, c: (b, c, ODD_LORA_BLOCK)), sh_spec, st_spec,
                  vec(RW_COLS), vec(D), mat, vec(D), mat, mat, vec(D), vec(D), vec(D), vec(D), vec(D)],
        out_specs=[pl.BlockSpec((1, L, D), lambda b, c: (b, c, 0)), sh_spec, st_spec],
        out_shape=[jax.ShapeDtypeStruct((B, T, D), F32),
                   jax.ShapeDtypeStruct((B, SUBLANES, RW_COLS), F32),
                   jax.ShapeDtypeStruct((B, H // 2, N, 2 * N), F32)],
        scratch_shapes=[pltpu.VMEM((SUBLANES + L, RW_COLS), F32), pltpu.VMEM((H // 2, N, 2 * N), F32),
                        rows, rows, rows, rows, rows, cols, cols, pltpu.VMEM((L, D), F32)],
        compiler_params=_cparams(("arbitrary", "arbitrary"), 48),
        name="rwkv_mixer",
    )(proj, proj, shift0, st0, mu.reshape(1, RW_COLS), w0.reshape(1, D), w2p, a0.reshape(1, D), a2p, g2,
      k_k.reshape(1, D), k_a.reshape(1, D), r_k.reshape(1, D), ln_g.reshape(1, D), ln_b.reshape(1, D))
    new_st = jnp.swapaxes(sto.reshape(B, H // 2, N, 2, N), 2, 3).reshape(B, H, N, N)
    return y, shifto[:, SUBLANES - 1], new_st


def _logf_kernel(fl_ref, bias_ref, o_ref):
    x = fl_ref[0] + bias_ref[...]
    lf = jnp.minimum(x, 0.0) - jnp.log1p(jnp.exp(-jnp.abs(x)))
    o_ref[0] = lf[:, 0:FOX_HEADS]


def fox_logf(proj, f_bias):
    B, T, _ = proj.shape
    tt = min(T, 512)
    return pl.pallas_call(
        _logf_kernel,
        grid=(B, T // tt),
        in_specs=[pl.BlockSpec((1, tt, LANES), lambda b, i: (b, i, ODD_FL_BLOCK)),
                  pl.BlockSpec((1, LANES), lambda b, i: (0, 0))],
        out_specs=pl.BlockSpec((1, tt, FOX_HEADS), lambda b, i: (b, i, 0)),
        out_shape=jax.ShapeDtypeStruct((B, T, FOX_HEADS), F32),
        compiler_params=_cparams(("arbitrary", "arbitrary")),
        name="fox_logf",
    )(proj, jnp.pad(f_bias.reshape(1, FOX_HEADS), ((0, 0), (0, LANES - FOX_HEADS))))


def _cumsum_kernel(x_ref, o_ref, carry_ref):
    ts = x_ref.shape[1]

    @pl.when(pl.program_id(1) == 0)
    def _():
        carry_ref[...] = jnp.zeros(carry_ref.shape, F32)

    row = lax.broadcasted_iota(jnp.int32, (ts, ts), 0)
    col = lax.broadcasted_iota(jnp.int32, (ts, ts), 1)
    cum = _mm_hi((col <= row).astype(F32), x_ref[0]) + carry_ref[...]
    o_ref[0] = cum
    carry_ref[...] = cum[ts - 1:ts, :]


def time_cumsum(x):
    B, S, H = x.shape
    ts = 512 if S % 512 == 0 else LANES
    assert S % ts == 0
    return pl.pallas_call(
        _cumsum_kernel,
        grid=(B, S // ts),
        in_specs=[pl.BlockSpec((1, ts, H), lambda b, i: (b, i, 0))],
        out_specs=pl.BlockSpec((1, ts, H), lambda b, i: (b, i, 0)),
        out_shape=jax.ShapeDtypeStruct((B, S, H), F32),
        scratch_shapes=[pltpu.VMEM((1, H), F32)],
        compiler_params=_cparams(("arbitrary", "arbitrary")),
        name="time_cumsum",
    )(x)


def _aug_placement(mode):
    h = jnp.arange(LANES)[:, None]
    c = jnp.arange(FOX_AW)[None, :]
    base = h * LANES + FOX_HD
    valid = h < FOX_HEADS
    term_off = 0 if mode == 'q' else 3
    sign = 1.0 if mode == 'q' else -1.0
    place = jnp.stack([jnp.where(valid & (c == base + term_off + i), sign, 0.0) for i in range(3)]).astype(BF16)
    lane = jnp.arange(FOX_AW) % LANES
    if mode == 'q':
        ones = (lane >= FOX_HD + 3) & (lane < FOX_HD + 6)
    elif mode == 'k':
        ones = (lane >= FOX_HD) & (lane < FOX_HD + 3)
    else:
        ones = lane == FOX_HD
    return place, jnp.where(ones, 1.0, 0.0).reshape(1, FOX_AW).astype(F32)


def _fox_aug_kernel(x_ref, cum_ref, place_ref, ones_ref, o_ref, *, mode):
    tt = x_ref.shape[1]
    x = x_ref[0]
    if mode == 'q':
        x = x * (FOX_HD ** -0.5 * LOG2E)
    aug = ones_ref[...]
    if mode != 'v':
        c = cum_ref[0] * LOG2E
        c1 = c.astype(BF16)
        r1 = c - c1.astype(F32)
        c2 = r1.astype(BF16)
        c3 = (r1 - c2.astype(F32)).astype(BF16)
        aug = aug + (jnp.dot(c1, place_ref[0], preferred_element_type=F32)
                     + jnp.dot(c2, place_ref[1], preferred_element_type=F32)
                     + jnp.dot(c3, place_ref[2], preferred_element_type=F32))
    else:
        aug = jnp.broadcast_to(aug, (tt, FOX_AW))
    lo = lax.broadcasted_iota(jnp.int32, (tt, LANES), 1) < FOX_HD
    for m in range(FOX_HEADS // 2):
        x_m = x[:, LANES * m:LANES * (m + 1)]
        even = jnp.where(lo, x_m, aug[:, 2 * LANES * m:2 * LANES * m + LANES])
        odd = jnp.where(lo, pltpu.roll(x_m, FOX_HD, 1), aug[:, 2 * LANES * m + LANES:2 * LANES * (m + 1)])
        o_ref[0, :, 2 * LANES * m:2 * LANES * (m + 1)] = jnp.concatenate([even, odd], axis=1).astype(BF16)


def fox_augment(arr, col_block, cum, mode):
    B, S, _ = arr.shape
    tt = min(S, 256)
    if S % tt:
        tt = LANES
    place, ones = _aug_placement(mode)
    return pl.pallas_call(
        functools.partial(_fox_aug_kernel, mode=mode),
        grid=(B, S // tt),
        in_specs=[pl.BlockSpec((1, tt, D), lambda b, i: (b, i, col_block)),
                  pl.BlockSpec((1, tt, LANES), lambda b, i: (b, i, 0)),
                  pl.BlockSpec((3, LANES, FOX_AW), lambda b, i: (0, 0, 0)),
                  pl.BlockSpec((1, FOX_AW), lambda b, i: (0, 0))],
        out_specs=pl.BlockSpec((1, tt, FOX_AW), lambda b, i: (b, i, 0)),
        out_shape=jax.ShapeDtypeStruct((B, S, FOX_AW), BF16),
        compiler_params=_cparams(("arbitrary", "arbitrary")),
        name="fox_augment_" + mode,
    )(arr, cum, place, ones)


def _fox_kernel(qa_ref, ka_ref, va_ref, o_ref, m_ref, acc_ref, *, past, tq, tk, ts):
    i = pl.program_id(2)
    j = pl.program_id(3)

    @pl.when(j == 0)
    def _():
        m_ref[...] = jnp.full(m_ref.shape, NEG_INF, F32)
        acc_ref[...] = jnp.zeros(acc_ref.shape, F32)

    first_q = past + i * tq
    needed = j * tk <= first_q + tq - 1
    unmasked = (j + 1) * tk - 1 <= first_q

    def tile(masked):
        if masked:
            diff = (lax.broadcasted_iota(jnp.int32, (ts, tk), 1)
                    - lax.broadcasted_iota(jnp.int32, (ts, tk), 0))
        work = [(hh, r) for hh in range(2) for r in range(tq // ts)]
        scores = []
        for hh, r in work:
            hs = slice(LANES * hh, LANES * (hh + 1))
            s = lax.dot_general(qa_ref[0, ts * r:ts * (r + 1), hs], ka_ref[0, :, hs], (((1,), (1,)), ((), ())),
                                preferred_element_type=F32)
            if masked:
                s = jnp.where(diff <= first_q + ts * r - j * tk, s, NEG_INF)
            scores.append(s)
        for (hh, r), s in zip(work, scores):
            hs = slice(LANES * hh, LANES * (hh + 1))
            rows = slice(ts * r, ts * (r + 1))
            chunks = [s[:, LANES * c:LANES * (c + 1)] for c in range(tk // LANES)]
            smax = functools.reduce(jnp.maximum, chunks)
            m_old = m_ref[hh, rows]
            m_new = jnp.maximum(m_old, jnp.max(smax, axis=1, keepdims=True))
            pm = jnp.concatenate([jnp.exp2(ch - m_new) for ch in chunks], axis=1)
            acc_ref[hh, rows] = (jnp.exp2(m_old - m_new) * acc_ref[hh, rows]
                                 + jnp.dot(pm.astype(BF16), va_ref[0, :, hs], preferred_element_type=F32))
            m_ref[hh, rows] = m_new

    @pl.when(jnp.logical_and(needed, unmasked))
    def _():
        tile(False)

    @pl.when(jnp.logical_and(needed, jnp.logical_not(unmasked)))
    def _():
        tile(True)

    @pl.when(j == pl.num_programs(3) - 1)
    def _():
        lo = lax.broadcasted_iota(jnp.int32, (tq, LANES), 1) < FOX_HD
        a0 = acc_ref[0]
        a1 = acc_ref[1]
        o0 = a0 / a0[:, FOX_HD:FOX_HD + 1]
        o1 = a1 / a1[:, FOX_HD:FOX_HD + 1]
        o_ref[0] = jnp.where(lo, o0, pltpu.roll(o1, FOX_HD, 1))


def fox_attention(qa, ka, va, past, T):
    B, S, _ = ka.shape
    tq = min(T, 512)
    tk = 512 if S % 512 == 0 else S
    ts = tq
    nq, nk = T // tq, S // tk
    last = lambda i: (past + (i + 1) * tq - 1) // tk
    kj = lambda i, j: jnp.minimum(j, last(i))
    return pl.pallas_call(
        functools.partial(_fox_kernel, past=past, tq=tq, tk=tk, ts=ts),
        grid=(B, FOX_HEADS // 2, nq, nk),
        in_specs=[
            pl.BlockSpec((1, tq, 2 * LANES), lambda b, p, i, j: (b, i, p)),
            pl.BlockSpec((1, tk, 2 * LANES), lambda b, p, i, j: (b, kj(i, j), p)),
            pl.BlockSpec((1, tk, 2 * LANES), lambda b, p, i, j: (b, kj(i, j), p)),
        ],
        out_specs=pl.BlockSpec((1, tq, LANES), lambda b, p, i, j: (b, i, p)),
        out_shape=jax.ShapeDtypeStruct((B, T, D), F32),
        scratch_shapes=[pltpu.VMEM((2, tq, LANES), F32), pltpu.VMEM((2, tq, LANES), F32)],
        compiler_params=_cparams(("arbitrary", "arbitrary", "arbitrary", "arbitrary"), 48),
        name="fox_attention",
    )(qa, ka, va)


def _trunk(x, mods, prm, wts, conv_st, ssd_st, hg_st, shift_st, rw_st, fk_past, fv_past, fl_past):
    B, T, _ = x.shape
    n = B * T
    sh1, sc1, gt1, sh2, sc2, gt2 = mods[0]
    proj = norm_inproj(x, prm['norm_mix_g'][0], sc1, sh1, wts['even_in'])
    ya, n_conv, n_ssd = ssd_mixer(proj, conv_st, ssd_st, prm['ssd_conv_w'][0], prm['ssd_conv_b'][0],
                                  prm['ssd_dt_bias'][0], prm['ssd_a_log'][0], prm['ssd_d'][0],
                                  prm['ssd_norm_g'][0])
    yb, n_hg = hgrn_mixer(proj, hg_st, wts['hgrn_lb'], prm['hgrn_norm_g'][0])
    x, hn = mix_outproj(ya, yb, wts['even_out'], x, gt1, prm['norm_ffn_g'][0], sc2, sh2)
    hn2 = hn.reshape(n, D)
    gates = moe_router(hn2, prm['moe_router_w'][0], prm['moe_router_bias'][0])
    ffn = moe_experts(hn2, gates, 0, prm['moe_w_gate'], prm['moe_w_up'], prm['moe_w_down'],
                      prm['moe_sh_gate'], prm['moe_sh_up'], prm['moe_sh_down'])
    x = residual(x, ffn, gt2, prm['final_norm_g'], False)

    sh1, sc1, gt1, sh2, sc2, gt2 = mods[1]
    proj = norm_inproj(x, prm['norm_mix_g'][1], sc1, sh1, wts['odd_in'])
    yc, n_shift, n_rw = rwkv_mixer(proj, shift_st, rw_st, prm['rwkv_mu'][0], prm['rwkv_w0'][0], prm['rwkv_w2'][0],
                                   prm['rwkv_a0'][0], prm['rwkv_a2'][0], prm['rwkv_g2'][0], prm['rwkv_k_k'][0],
                                   prm['rwkv_k_a'][0], prm['rwkv_r_k'][0], prm['rwkv_ln_g'][0],
                                   prm['rwkv_ln_b'][0])
    logf = fox_logf(proj, prm['fox_f_bias'][0])
    kf = proj[:, :, ODD_K_BLOCK * D:(ODD_K_BLOCK + 1) * D]
    vf = proj[:, :, ODD_V_BLOCK * D:(ODD_V_BLOCK + 1) * D]
    lane_pad = lambda a: jnp.pad(a, ((0, 0), (0, 0), (0, LANES - FOX_HEADS)))
    if fk_past is None:
        past = 0
        cum = time_cumsum(lane_pad(logf))
        ka = fox_augment(proj, ODD_K_BLOCK, cum, 'k')
        va = fox_augment(proj, ODD_V_BLOCK, cum, 'v')
    else:
        past = fk_past.shape[1]
        pad = (-(past + T)) % LANES
        lf_all = jnp.concatenate([fl_past, logf, jnp.zeros((B, pad, FOX_HEADS), F32)], axis=1)
        cum = time_cumsum(lane_pad(lf_all))
        k_arr = jnp.concatenate([fk_past.reshape(B, past, D), kf, jnp.zeros((B, pad, D), F32)], axis=1)
        v_arr = jnp.concatenate([fv_past.reshape(B, past, D), vf, jnp.zeros((B, pad, D), F32)], axis=1)
        ka = fox_augment(k_arr, 0, cum, 'k')
        va = fox_augment(v_arr, 0, cum, 'v')
    qa = fox_augment(proj, ODD_Q_BLOCK, cum[:, past:past + T], 'q')
    yd = fox_attention(qa, ka, va, past, T)
    x, hn = mix_outproj(yc, yd, wts['odd_out'], x, gt1, prm['norm_ffn_g'][1], sc2, sh2)
    hn2 = hn.reshape(n, D)
    gates = moe_router(hn2, prm['moe_router_w'][1], prm['moe_router_bias'][1])
    ffn = moe_experts(hn2, gates, 1, prm['moe_w_gate'], prm['moe_w_up'], prm['moe_w_down'],
                      prm['moe_sh_gate'], prm['moe_sh_up'], prm['moe_sh_down'])
    y = residual(x, ffn, gt2, prm['final_norm_g'], True)
    hs = (B, T, FOX_HEADS, FOX_HD)
    return (y, n_conv[None], n_ssd[None], n_hg[None], n_shift[None], n_rw[None],
            kf.reshape(hs)[None], vf.reshape(hs)[None], logf[None])


def _prep_weights(prm):
    ein = prm['even_in_w'][0]
    z, xbc, dt, hq, hf, hi, hgate = jnp.split(ein, [1024, 2304, 2320, 3344, 4368, 5392], axis=1)
    even_in = jnp.concatenate([z, hq, hf, hi, hgate, xbc, dt, jnp.zeros((D, LANES - SSD_HEADS), F32)], axis=1)
    rkv, lora, qkv, fl = jnp.split(prm['odd_in_w'][0], [3 * D, RW_COLS, RW_COLS + 3 * D], axis=1)
    odd_in = jnp.concatenate([rkv, qkv, lora, fl, jnp.zeros((D, LANES - FOX_HEADS), F32)], axis=1)
    lb = jnp.cumsum(jax.nn.softmax(prm['hgrn_lb_logits'].astype(F32), axis=0), axis=0)[0]
    return {
        'even_in': even_in.astype(BF16),
        'odd_in': odd_in.astype(BF16),
        'even_out': prm['even_out_w'][0].astype(BF16),
        'odd_out': prm['odd_out_w'][0].astype(BF16),
        'hgrn_lb': lb,
    }


def kernel(x_prompt, x_sample, c_prompt, c_sample, state_ssd_conv, state_ssd, state_hgrn, state_rwkv_shift, state_rwkv, cache_fox_k, cache_fox_v, cache_fox_logf, ada_w, ada_b, norm_mix_g, norm_ffn_g, final_norm_g, even_in_w, even_out_w, ssd_conv_w, ssd_conv_b, ssd_dt_bias, ssd_a_log, ssd_d, ssd_norm_g, hgrn_lb_logits, hgrn_norm_g, odd_in_w, odd_out_w, rwkv_mu, rwkv_w0, rwkv_w2, rwkv_a0, rwkv_a2, rwkv_g2, rwkv_k_k, rwkv_k_a, rwkv_r_k, rwkv_ln_g, rwkv_ln_b, fox_f_bias, moe_router_w, moe_router_bias, moe_w_gate, moe_w_up, moe_w_down, moe_sh_gate, moe_sh_up, moe_sh_down):
    prm = dict(ada_w=ada_w, ada_b=ada_b, norm_mix_g=norm_mix_g, norm_ffn_g=norm_ffn_g, final_norm_g=final_norm_g,
               even_in_w=even_in_w, even_out_w=even_out_w, ssd_conv_w=ssd_conv_w, ssd_conv_b=ssd_conv_b,
               ssd_dt_bias=ssd_dt_bias, ssd_a_log=ssd_a_log, ssd_d=ssd_d, ssd_norm_g=ssd_norm_g,
               hgrn_lb_logits=hgrn_lb_logits, hgrn_norm_g=hgrn_norm_g,
               odd_in_w=odd_in_w, odd_out_w=odd_out_w, rwkv_mu=rwkv_mu, rwkv_w0=rwkv_w0, rwkv_w2=rwkv_w2,
               rwkv_a0=rwkv_a0, rwkv_a2=rwkv_a2, rwkv_g2=rwkv_g2, rwkv_k_k=rwkv_k_k, rwkv_k_a=rwkv_k_a,
               rwkv_r_k=rwkv_r_k, rwkv_ln_g=rwkv_ln_g, rwkv_ln_b=rwkv_ln_b, fox_f_bias=fox_f_bias,
               moe_router_w=moe_router_w, moe_router_bias=moe_router_bias, moe_w_gate=moe_w_gate,
               moe_w_up=moe_w_up, moe_w_down=moe_w_down, moe_sh_gate=moe_sh_gate, moe_sh_up=moe_sh_up,
               moe_sh_down=moe_sh_down)
    wts = _prep_weights(prm)
    bp, bs = x_prompt.shape[0], x_sample.shape[0]
    depth = ada_w.shape[0]
    c_all = jnp.concatenate([c_prompt, c_sample], axis=0)
    rows = c_all.shape[0]
    c_all = jnp.pad(c_all, ((0, (-rows) % SUBLANES), (0, 0)))
    mod = ada_mod(c_all, ada_w, ada_b)

    def mods_for(lo, hi):
        return [tuple(m[:, None, :] for m in jnp.split(mod[l, lo:hi], 6, axis=-1)) for l in range(depth)]

    z = lambda *s: jnp.zeros(s, F32)
    out_p = _trunk(x_prompt, mods_for(0, bp), prm, wts,
                   z(bp, SSD_CONV - 1, SSD_CONV_DIM), z(bp, SSD_HEADS, SSD_HD, SSD_STATE),
                   z(bp, HG_HEADS, HG_DK, HG_DV), z(bp, RW_COLS), z(bp, RW_HEADS, RW_N, RW_N),
                   None, None, None)
    out_s = _trunk(x_sample, mods_for(bp, bp + bs), prm, wts,
                   state_ssd_conv[0], state_ssd[0], state_hgrn[0], state_rwkv_shift[0], state_rwkv[0],
                   cache_fox_k[0], cache_fox_v[0], cache_fox_logf[0])
    return (out_p[0], out_s[0]) + tuple(out_p[1:]) + tuple(out_s[1:])
```

```python
import functools

import jax
import jax.numpy as jnp
from jax import lax
from jax.experimental import pallas as pl
from jax.experimental.pallas import tpu as pltpu

F32 = jnp.float32
BF16 = jnp.bfloat16
HI = lax.Precision.HIGHEST

D = 1024
EPS = 1e-6
LANES = 128
SUBLANES = 8

SSD_HEADS = 16
SSD_HD = 64
SSD_STATE = 64
SSD_CONV = 4
SSD_CONV_DIM = 1280
SSD_L = 64
HG_HEADS = 8
HG_DK = 128
HG_DV = 128
HG_L = 16
RW_HEADS = 16
RW_N = 64
RW_COLS = 3328
RW_LN_EPS = 64e-5
RW_L = 64
FOX_HEADS = 16
FOX_HD = 64
FOX_AW = FOX_HEADS * LANES
FOX_TK = 1024
LOG2E = 1.4426950408889634
ODD_Q_BLOCK, ODD_K_BLOCK, ODD_V_BLOCK = 3, 4, 5
ODD_LORA_BLOCK = 24
ODD_FL_BLOCK = 50
RW_LORA = 256
N_EXPERTS = 64
TOP_K = 6
N_GROUPS = 8
TOPK_GROUPS = 4
D_EXPERT = 256
ROUTED_SCALE = 2.5
PROJ_COLS = 6528
PROJ_TN = 2176

NEG_INF = float("-inf")


def _cparams(sem, vmem_mb=None):
    kw = dict(dimension_semantics=sem)
    if vmem_mb is not None:
        kw["vmem_limit_bytes"] = vmem_mb * 1024 * 1024
    return pltpu.CompilerParams(**kw)


def _mm(a, b):
    return jnp.dot(a.astype(BF16), b.astype(BF16), preferred_element_type=F32)


def _mm_nt(a, b):
    return lax.dot_general(a.astype(BF16), b.astype(BF16), (((1,), (1,)), ((), ())),
                           preferred_element_type=F32)


def _mm_hi(a, b):
    return jnp.dot(a, b, precision=HI, preferred_element_type=F32)


def _sigmoid(x):
    return 1.0 / (1.0 + jnp.exp(-x))


def _silu(x):
    return x * _sigmoid(x)


def _softplus(x):
    return jnp.maximum(x, 0.0) + jnp.log1p(jnp.exp(-jnp.abs(x)))


def _row_tiles(G, gs, target):
    if gs >= target:
        assert gs % target == 0
        return 1, target
    gb = max(1, min(G, target // gs))
    while G % gb:
        gb -= 1
    return gb, gs


def _ada_kernel(c_ref, w_ref, b_ref, o_ref):
    c = c_ref[...]
    o_ref[0] = _mm(_silu(c), w_ref[0]) + b_ref[0]


def ada_mod(c_all, ada_w, ada_b):
    depth = ada_w.shape[0]
    bp = c_all.shape[0]
    tn = 1024
    return pl.pallas_call(
        _ada_kernel,
        grid=(depth, 6 * D // tn),
        in_specs=[
            pl.BlockSpec((bp, D), lambda l, j: (0, 0)),
            pl.BlockSpec((1, D, tn), lambda l, j: (l, 0, j)),
            pl.BlockSpec((1, 1, tn), lambda l, j: (l, 0, j)),
        ],
        out_specs=pl.BlockSpec((1, bp, tn), lambda l, j: (l, 0, j)),
        out_shape=jax.ShapeDtypeStruct((depth, bp, 6 * D), F32),
        compiler_params=_cparams(("arbitrary", "arbitrary")),
        name="ada_mod",
    )(c_all, ada_w, ada_b.reshape(depth, 1, 6 * D))


def _rms_mod(x, g, sc, sh):
    y = x * lax.rsqrt(jnp.mean(x * x, axis=-1, keepdims=True) + EPS)
    return (y * g) * (1.0 + sc) + sh


def _inproj_kernel(x_ref, g_ref, sc_ref, sh_ref, w_ref, o_ref, xn_ref):
    @pl.when(pl.program_id(1) == 0)
    def _():
        hn = _rms_mod(x_ref[...], g_ref[...], sc_ref[...], sh_ref[...])
        xn_ref[...] = hn.reshape(xn_ref.shape).astype(BF16)

    o_ref[...] = jnp.dot(xn_ref[...], w_ref[...], preferred_element_type=F32)


def norm_inproj(x, g, sc, sh, w_bf):
    G, gs, _ = x.shape
    gb, tb = _row_tiles(G, gs, 512)
    nt = gs // tb
    tt = gb * tb
    n = G * gs
    out = pl.pallas_call(
        _inproj_kernel,
        grid=(n // tt, PROJ_COLS // PROJ_TN),
        in_specs=[
            pl.BlockSpec((gb, tb, D), lambda i, j: (i // nt, i % nt, 0)),
            pl.BlockSpec((1, D), lambda i, j: (0, 0)),
            pl.BlockSpec((gb, 1, D), lambda i, j: (i // nt, 0, 0)),
            pl.BlockSpec((gb, 1, D), lambda i, j: (i // nt, 0, 0)),
            pl.BlockSpec((D, PROJ_TN), lambda i, j: (0, j)),
        ],
        out_specs=pl.BlockSpec((tt, PROJ_TN), lambda i, j: (i, j)),
        out_shape=jax.ShapeDtypeStruct((n, PROJ_COLS), F32),
        scratch_shapes=[pltpu.VMEM((tt, D), BF16)],
        compiler_params=_cparams(("arbitrary", "arbitrary"), 48),
        name="norm_inproj",
    )(x, g.reshape(1, D), sc, sh, w_bf)
    return out.reshape(G, gs, PROJ_COLS)


def _outproj_kernel(ya_ref, yb_ref, w_ref, x_ref, gt_ref, g_ref, sc_ref, sh_ref, xo_ref, hn_ref):
    gb, tb, _ = x_ref.shape
    out = (jnp.dot(ya_ref[...].astype(BF16), w_ref[0:D, :], preferred_element_type=F32)
           + jnp.dot(yb_ref[...].astype(BF16), w_ref[D:2 * D, :], preferred_element_type=F32))
    x = x_ref[...] + gt_ref[...] * out.reshape(gb, tb, D)
    xo_ref[...] = x
    hn_ref[...] = _rms_mod(x, g_ref[...], sc_ref[...], sh_ref[...])


def mix_outproj(ya, yb, w_bf, x, gt, g, sc, sh):
    G, gs, _ = x.shape
    gb, tb = _row_tiles(G, gs, 512)
    nt = gs // tb
    tt = gb * tb
    n = G * gs
    row3 = pl.BlockSpec((gb, tb, D), lambda i: (i // nt, i % nt, 0))
    mod3 = pl.BlockSpec((gb, 1, D), lambda i: (i // nt, 0, 0))
    return pl.pallas_call(
        _outproj_kernel,
        grid=(n // tt,),
        in_specs=[
            pl.BlockSpec((tt, D), lambda i: (i, 0)),
            pl.BlockSpec((tt, D), lambda i: (i, 0)),
            pl.BlockSpec((2 * D, D), lambda i: (0, 0)),
            row3, mod3,
            pl.BlockSpec((1, D), lambda i: (0, 0)),
            mod3, mod3,
        ],
        out_specs=[row3, row3],
        out_shape=[jax.ShapeDtypeStruct((G, gs, D), F32), jax.ShapeDtypeStruct((G, gs, D), F32)],
        compiler_params=_cparams(("arbitrary",), 48),
        name="mix_outproj",
    )(ya.reshape(n, D), yb.reshape(n, D), w_bf, x, gt, g.reshape(1, D), sc, sh)


def _ssd_kernel(z_ref, xbc_ref, dt_ref, conv0_ref, st0_ref, cw_ref, cb_ref, dtb_ref, alog_ref, dfull_ref,
                ng_ref, y_ref, convo_ref, sto_ref, ext_ref, st_ref, yacc_ref):
    L = SSD_L
    c = pl.program_id(1)

    @pl.when(c == 0)
    def _():
        ext_ref[0:SUBLANES, :] = conv0_ref[0]
        st_ref[...] = st0_ref[0]

    u = xbc_ref[0]
    ext_ref[SUBLANES:SUBLANES + L, :] = u
    acc = cb_ref[...] + u * cw_ref[SSD_CONV - 1:SSD_CONV, :]
    for i in range(SSD_CONV - 1):
        sh = SSD_CONV - 1 - i
        acc = acc + ext_ref[SUBLANES - sh:SUBLANES - sh + L, :] * cw_ref[i:i + 1, :]
    ext_ref[0:SUBLANES, :] = ext_ref[L:L + SUBLANES, :]
    xbc = _silu(acc)
    xs = xbc[:, 0:D]
    bm2 = xbc[:, D:D + LANES]
    cm2 = xbc[:, D + LANES:D + 2 * LANES]

    dt = _softplus(dt_ref[0] + dtb_ref[...])
    a = dt * (-jnp.exp(alog_ref[...]))
    row = lax.broadcasted_iota(jnp.int32, (L, L), 0)
    col = lax.broadcasted_iota(jnp.int32, (L, L), 1)
    tril = (col <= row).astype(F32)
    triu = (row <= col).astype(F32)
    cum = _mm_hi(tril, a)
    cum_t = _mm_hi(a.T, triu)
    eh = lax.broadcasted_iota(jnp.int32, (LANES, D), 0)
    ec = lax.broadcasted_iota(jnp.int32, (LANES, D), 1)
    expand = (ec // SSD_HD == eh).astype(F32)
    dt_full = _mm_hi(dt, expand)
    cum_full = _mm_hi(cum, expand)
    ecum_full = jnp.exp(cum_full)
    last_full = cum_full[L - 1:L, :]
    w_full = jnp.exp(last_full - cum_full)
    xdt = xs * dt_full

    lane = lax.broadcasted_iota(jnp.int32, (L, LANES), 1)
    lo = lane < SSD_HD
    tri2 = (lane % L) <= lax.broadcasted_iota(jnp.int32, (L, LANES), 0)
    bd_r = lax.broadcasted_iota(jnp.int32, (LANES, LANES), 0) // SSD_HD
    bd_c = lax.broadcasted_iota(jnp.int32, (LANES, LANES), 1) // SSD_HD
    blockdiag = bd_r == bd_c
    bm_r = pltpu.roll(bm2, SSD_HD, 1)
    cm_r = pltpu.roll(cm2, SSD_HD, 1)
    bmstack = jnp.concatenate([bm2, bm2], axis=0)

    for g in range(2):
        grp_lo = lo if g == 0 else jnp.logical_not(lo)
        cm_m = jnp.where(grp_lo, cm2, 0.0)
        cb_dup = _mm_nt(cm_m, bmstack)
        cm_dup = jnp.where(grp_lo, cm2, cm_r)
        bm_dup = jnp.where(grp_lo, bm2, bm_r)
        for pp in range(4):
            p = 4 * g + pp
            sl = slice(LANES * p, LANES * (p + 1))
            cum_pair = cum_full[:, sl]
            cum_row = jnp.concatenate([cum_t[2 * p:2 * p + 1, :], cum_t[2 * p + 1:2 * p + 2, :]], axis=1)
            s2 = cb_dup * jnp.exp(jnp.where(tri2, cum_pair - cum_row, NEG_INF))
            x2 = xdt[:, sl]
            xstack = jnp.concatenate([jnp.where(lo, x2, 0.0), jnp.where(lo, 0.0, x2)], axis=0)
            hp = st_ref[p]
            lhs = jnp.concatenate([s2, cm_dup * ecum_full[:, sl]], axis=1)
            rhs = jnp.concatenate([xstack, hp], axis=0)
            yacc_ref[:, sl] = _mm(lhs, rhs)
            bw_t = (bm_dup * w_full[:, sl]).T
            upd = _mm(bw_t, x2)
            st_ref[p] = jnp.where(blockdiag, hp * ecum_full[L - 1:L, sl] + upd, 0.0)

    y = yacc_ref[...] + xs * dfull_ref[...]
    z = z_ref[0]
    yz = y * _silu(z)
    y_ref[0] = yz * lax.rsqrt(jnp.mean(yz * yz, axis=-1, keepdims=True) + EPS) * ng_ref[...]

    @pl.when(c == pl.num_programs(1) - 1)
    def _():
        convo_ref[0] = ext_ref[0:SUBLANES, :]
        sto_ref[0] = st_ref[...]


def ssd_mixer(proj, conv_state, ssd_state, conv_w, conv_b, dt_bias, a_log, d_skip, norm_g):
    B, T, _ = proj.shape
    L = SSD_L
    assert T % L == 0
    H, P, N = SSD_HEADS, SSD_HD, SSD_STATE
    conv0 = jnp.pad(conv_state, ((0, 0), (SUBLANES - (SSD_CONV - 1), 0), (0, 0)))
    st_t = jnp.swapaxes(ssd_state, -1, -2).reshape(B, H // 2, 2, N, P)
    st0 = jnp.zeros((B, H // 2, 2, N, 2, P), F32)
    st0 = st0.at[:, :, 0, :, 0, :].set(st_t[:, :, 0]).at[:, :, 1, :, 1, :].set(st_t[:, :, 1])
    st0 = st0.reshape(B, H // 2, 2 * N, 2 * P)
    cw = jnp.pad(conv_w, ((0, SUBLANES - SSD_CONV), (0, 0)))
    pad16 = lambda v: jnp.pad(v.reshape(1, H), ((0, 0), (0, LANES - H)))
    dfull = jnp.repeat(d_skip, P).reshape(1, D)
    vec = lambda w: pl.BlockSpec((1, w), lambda b, c: (0, 0))
    y, convo, sto = pl.pallas_call(
        _ssd_kernel,
        grid=(B, T // L),
        in_specs=[
            pl.BlockSpec((1, L, D), lambda b, c: (b, c, 0)),
            pl.BlockSpec((1, L, SSD_CONV_DIM), lambda b, c: (b, c, 4)),
            pl.BlockSpec((1, L, LANES), lambda b, c: (b, c, 50)),
            pl.BlockSpec((1, SUBLANES, SSD_CONV_DIM), lambda b, c: (b, 0, 0)),
            pl.BlockSpec((1, H // 2, LANES, LANES), lambda b, c: (b, 0, 0, 0)),
            pl.BlockSpec((SUBLANES, SSD_CONV_DIM), lambda b, c: (0, 0)),
            vec(SSD_CONV_DIM), vec(LANES), vec(LANES), vec(D), vec(D),
        ],
        out_specs=[
            pl.BlockSpec((1, L, D), lambda b, c: (b, c, 0)),
            pl.BlockSpec((1, SUBLANES, SSD_CONV_DIM), lambda b, c: (b, 0, 0)),
            pl.BlockSpec((1, H // 2, LANES, LANES), lambda b, c: (b, 0, 0, 0)),
        ],
        out_shape=[
            jax.ShapeDtypeStruct((B, T, D), F32),
            jax.ShapeDtypeStruct((B, SUBLANES, SSD_CONV_DIM), F32),
            jax.ShapeDtypeStruct((B, H // 2, LANES, LANES), F32),
        ],
        scratch_shapes=[
            pltpu.VMEM((SUBLANES + L, SSD_CONV_DIM), F32),
            pltpu.VMEM((H // 2, LANES, LANES), F32),
            pltpu.VMEM((L, D), F32),
        ],
        compiler_params=_cparams(("arbitrary", "arbitrary"), 48),
        name="ssd_mixer",
    )(proj, proj, proj, conv0, st0, cw, conv_b.reshape(1, SSD_CONV_DIM), pad16(dt_bias), pad16(a_log),
      dfull, norm_g.reshape(1, D))
    sto = sto.reshape(B, H // 2, 2, N, 2, P)
    new_st = jnp.stack([sto[:, :, 0, :, 0, :], sto[:, :, 1, :, 1, :]], axis=2).reshape(B, H, N, P)
    return y, convo[:, SUBLANES - (SSD_CONV - 1):], jnp.swapaxes(new_st, -1, -2)


def _hgrn_kernel(q_ref, f_ref, i_ref, gate_ref, st0_ref, lb_ref, ng_ref, y_ref, sto_ref, st_ref):
    L = HG_L
    c = pl.program_id(1)

    @pl.when(c == 0)
    def _():
        st_ref[...] = st0_ref[0]

    lb = lb_ref[...]
    f = lb + (1.0 - lb) * _sigmoid(f_ref[0])
    logf = jnp.log(f)
    q = _silu(q_ref[0]) * (HG_DK ** -0.5)
    k = 1.0 - f
    v = i_ref[0]
    row = lax.broadcasted_iota(jnp.int32, (L, L), 0)
    col = lax.broadcasted_iota(jnp.int32, (L, L), 1)
    cum = _mm_hi((col <= row).astype(F32), logf)
    qe = q * jnp.exp(cum)
    last = cum[L - 1:L, :]
    kw = k * jnp.exp(last - cum)
    elast = jnp.exp(last)
    srow = lax.broadcasted_iota(jnp.int32, (L, HG_DK), 0)
    for h in range(HG_HEADS):
        sl = slice(HG_DK * h, HG_DK * (h + 1))
        gh, qh, kh, vh = cum[:, sl], q[:, sl], k[:, sl], v[:, sl]
        st = st_ref[h]
        inter = _mm_nt(qe[:, sl], st)
        rows = []
        for t in range(L):
            dec = jnp.exp(jnp.where(srow <= t, gh[t:t + 1, :] - gh, NEG_INF))
            sc = jnp.sum(qh[t:t + 1, :] * kh * dec, axis=1, keepdims=True)
            rows.append(jnp.sum(sc * vh, axis=0, keepdims=True))
        o = inter + jnp.concatenate(rows, axis=0)
        st_ref[h] = st * elast[:, sl] + _mm(vh.T, kw[:, sl])
        on = o * lax.rsqrt(jnp.mean(o * o, axis=-1, keepdims=True) + EPS) * ng_ref[:, sl]
        y_ref[0, :, sl] = on * _silu(gate_ref[0, :, sl])

    @pl.when(c == pl.num_programs(1) - 1)
    def _():
        sto_ref[0] = st_ref[...]


def hgrn_mixer(proj, hg_state, lb, norm_g):
    B, T, _ = proj.shape
    L = HG_L
    assert T % L == 0
    st0 = jnp.swapaxes(hg_state, -1, -2)
    col = lambda j: pl.BlockSpec((1, L, D), lambda b, c: (b, c, j))
    st_spec = pl.BlockSpec((1, HG_HEADS, HG_DV, HG_DK), lambda b, c: (b, 0, 0, 0))
    vec = pl.BlockSpec((1, D), lambda b, c: (0, 0))
    y, sto = pl.pallas_call(
        _hgrn_kernel,
        grid=(B, T // L),
        in_specs=[col(1), col(2), col(3), col(4), st_spec, vec, vec],
        out_specs=[pl.BlockSpec((1, L, D), lambda b, c: (b, c, 0)), st_spec],
        out_shape=[jax.ShapeDtypeStruct((B, T, D), F32),
                   jax.ShapeDtypeStruct((B, HG_HEADS, HG_DV, HG_DK), F32)],
        scratch_shapes=[pltpu.VMEM((HG_HEADS, HG_DV, HG_DK), F32)],
        compiler_params=_cparams(("arbitrary", "arbitrary")),
        name="hgrn_mixer",
    )(proj, proj, proj, proj, st0, lb.reshape(1, D), norm_g.reshape(1, D))
    return y, jnp.swapaxes(sto, -1, -2)


def _first_max_onehot(vals, iota, axis, size):
    m = jnp.max(vals, axis=axis, keepdims=True)
    idx = jnp.min(jnp.where(vals == m, iota, size), axis=axis, keepdims=True)
    return iota == idx, m


def _router_kernel(x_ref, wt_ref, bias_ref, o_ref):
    tt = x_ref.shape[0]
    per = N_EXPERTS // N_GROUPS
    logits = lax.dot_general(wt_ref[...], x_ref[...], (((1,), (1,)), ((), ())),
                             precision=HI, preferred_element_type=F32)
    scores = _sigmoid(logits)
    biased = scores + bias_ref[...]
    b3 = biased.reshape(N_GROUPS, per, tt)
    i3 = lax.broadcasted_iota(jnp.int32, (N_GROUPS, per, tt), 1)
    oh, m1 = _first_max_onehot(b3, i3, 1, per)
    m2 = jnp.max(jnp.where(oh, NEG_INF, b3), axis=1, keepdims=True)
    grp = (m1 + m2).reshape(N_GROUPS, tt)
    ig = lax.broadcasted_iota(jnp.int32, (N_GROUPS, tt), 0)
    gsel = jnp.zeros((N_GROUPS, tt), jnp.bool_)
    for _ in range(TOPK_GROUPS):
        oh, _m = _first_max_onehot(grp, ig, 0, N_GROUPS)
        gsel = jnp.logical_or(gsel, oh)
        grp = jnp.where(oh, NEG_INF, grp)
    gmask = jnp.where(gsel, 1.0, 0.0).reshape(N_GROUPS, 1, tt)
    masked = jnp.where(gmask > 0.0, b3, NEG_INF).reshape(N_EXPERTS, tt)
    ie = lax.broadcasted_iota(jnp.int32, (N_EXPERTS, tt), 0)
    esel = jnp.zeros((N_EXPERTS, tt), F32)
    for _ in range(TOP_K):
        oh, _m = _first_max_onehot(masked, ie, 0, N_EXPERTS)
        esel = jnp.where(oh, 1.0, esel)
        masked = jnp.where(oh, NEG_INF, masked)
    wsel = scores * esel
    gates = wsel / jnp.sum(wsel, axis=0, keepdims=True) * ROUTED_SCALE
    o_ref[...] = gates.T


def moe_router(hn2d, router_w, router_bias):
    n = hn2d.shape[0]
    tt = min(n, 512)
    return pl.pallas_call(
        _router_kernel,
        grid=(n // tt,),
        in_specs=[
            pl.BlockSpec((tt, D), lambda i: (i, 0)),
            pl.BlockSpec((N_EXPERTS, D), lambda i: (0, 0)),
            pl.BlockSpec((N_EXPERTS, 1), lambda i: (0, 0)),
        ],
        out_specs=pl.BlockSpec((tt, N_EXPERTS), lambda i: (i, 0)),
        out_shape=jax.ShapeDtypeStruct((n, N_EXPERTS), F32),
        compiler_params=_cparams(("arbitrary",)),
        name="moe_router",
    )(hn2d, router_w.T, router_bias.reshape(N_EXPERTS, 1))


def _ffn(xb, wg, wu, wd, gate_col):
    h = _silu(jnp.dot(xb, wg.astype(BF16), preferred_element_type=F32)) * jnp.dot(
        xb, wu.astype(BF16), preferred_element_type=F32)
    if gate_col is not None:
        h = h * gate_col
    return jnp.dot(h.astype(BF16), wd.astype(BF16), preferred_element_type=F32)


def _moe_kernel(x_ref, gates_ref, wg_ref, wu_ref, wd_ref, sg_ref, su_ref, sd_ref, o_ref, xb_ref, acc_ref):
    e = pl.program_id(1)

    @pl.when(e == 0)
    def _():
        xb_ref[...] = x_ref[...].astype(BF16)
        acc_ref[...] = _ffn(xb_ref[...], sg_ref[0], su_ref[0], sd_ref[0], None)

    gates = gates_ref[...]
    lane = lax.broadcasted_iota(jnp.int32, gates.shape, 1)
    gate_col = jnp.sum(jnp.where(lane == e, gates, 0.0), axis=1, keepdims=True)
    acc_ref[...] += _ffn(xb_ref[...], wg_ref[0, 0], wu_ref[0, 0], wd_ref[0, 0], gate_col)

    @pl.when(e == pl.num_programs(1) - 1)
    def _():
        o_ref[...] = acc_ref[...]


def moe_experts(hn2d, gates, l, w_gate, w_up, w_down, sh_gate, sh_up, sh_down):
    n = hn2d.shape[0]
    tt = min(n, 1024)
    return pl.pallas_call(
        _moe_kernel,
        grid=(n // tt, N_EXPERTS),
        in_specs=[
            pl.BlockSpec((tt, D), lambda i, e: (i, 0)),
            pl.BlockSpec((tt, N_EXPERTS), lambda i, e: (i, 0)),
            pl.BlockSpec((1, 1, D, D_EXPERT), lambda i, e: (l, e, 0, 0)),
            pl.BlockSpec((1, 1, D, D_EXPERT), lambda i, e: (l, e, 0, 0)),
            pl.BlockSpec((1, 1, D_EXPERT, D), lambda i, e: (l, e, 0, 0)),
            pl.BlockSpec((1, D, D_EXPERT), lambda i, e: (l, 0, 0)),
            pl.BlockSpec((1, D, D_EXPERT), lambda i, e: (l, 0, 0)),
            pl.BlockSpec((1, D_EXPERT, D), lambda i, e: (l, 0, 0)),
        ],
        out_specs=pl.BlockSpec((tt, D), lambda i, e: (i, 0)),
        out_shape=jax.ShapeDtypeStruct((n, D), F32),
        scratch_shapes=[pltpu.VMEM((tt, D), BF16), pltpu.VMEM((tt, D), F32)],
        compiler_params=_cparams(("arbitrary", "arbitrary"), 56),
        name="moe_experts",
    )(hn2d, gates, w_gate, w_up, w_down, sh_gate, sh_up, sh_down)


def _residual_kernel(x_ref, f_ref, gt_ref, g_ref, o_ref, *, final_norm):
    x = x_ref[...] + gt_ref[...] * f_ref[...]
    if final_norm:
        x = x * lax.rsqrt(jnp.mean(x * x, axis=-1, keepdims=True) + EPS) * g_ref[...]
    o_ref[...] = x


def residual(x, ffn, gt, g, final_norm):
    G, gs, _ = x.shape
    gb, tb = _row_tiles(G, gs, 512)
    nt = gs // tb
    row3 = pl.BlockSpec((gb, tb, D), lambda i: (i // nt, i % nt, 0))
    return pl.pallas_call(
        functools.partial(_residual_kernel, final_norm=final_norm),
        grid=(G * gs // (gb * tb),),
        in_specs=[row3, row3, pl.BlockSpec((gb, 1, D), lambda i: (i // nt, 0, 0)),
                  pl.BlockSpec((1, D), lambda i: (0, 0))],
        out_specs=row3,
        out_shape=jax.ShapeDtypeStruct((G, gs, D), F32),
        compiler_params=_cparams(("arbitrary",)),
        name="residual",
    )(x, ffn.reshape(G, gs, D), gt, g.reshape(1, D))


def _head_sum(x):
    L = x.shape[0]
    lo = lax.broadcasted_iota(jnp.int32, (L, LANES), 1) < RW_N
    outs = []
    for p in range(RW_HEADS // 2):
        x2 = x[:, LANES * p:LANES * (p + 1)]
        s_lo = jnp.sum(jnp.where(lo, x2, 0.0), axis=1, keepdims=True)
        s_hi = jnp.sum(jnp.where(lo, 0.0, x2), axis=1, keepdims=True)
        outs.append(jnp.where(lo, s_lo, s_hi))
    return jnp.concatenate(outs, axis=1)


def _rwkv_kernel(rkv_ref, lora_ref, shift0_ref, st0_ref, mu_ref, w0_ref, w2_ref, a0_ref, a2_ref, g2_ref, kk_ref, ka_ref,
                 rk_ref, lng_ref, lnb_ref, y_ref, shifto_ref, sto_ref,
                 ext_ref, st_ref, r_s, w_s, k_s, b_s, nkk_s, vt_s, ot_s, o_s):
    L = RW_L
    N = RW_N
    NP = RW_HEADS // 2
    c = pl.program_id(1)

    @pl.when(c == 0)
    def _():
        ext_ref[0:SUBLANES, :] = shift0_ref[0]
        st_ref[...] = st0_ref[0]

    pr = jnp.concatenate([rkv_ref[0], lora_ref[0]], axis=1)
    ext_ref[SUBLANES:SUBLANES + L, :] = pr
    prev = ext_ref[SUBLANES - 1:SUBLANES - 1 + L, :]
    ext_ref[0:SUBLANES, :] = ext_ref[L:L + SUBLANES, :]
    ps = pr + (prev - pr) * mu_ref[...]
    r = ps[:, 0:D]
    k = ps[:, D:2 * D]
    v = ps[:, 2 * D:3 * D]
    wa = ps[:, 3 * D:3 * D + LANES]
    gd = ps[:, 3 * D + LANES:3 * D + 2 * LANES]
    w = -_softplus(-(w0_ref[...] + _mm(jnp.tanh(wa), w2_ref[...]))) - 0.5
    wdec = jnp.exp(-jnp.exp(w))
    a = _sigmoid(a0_ref[...] + _mm(wa, a2_ref[...]))
    g = _mm(_sigmoid(gd), g2_ref[...])
    kk = k * kk_ref[...]
    kk = kk * lax.rsqrt(jnp.maximum(_head_sum(kk * kk), 1e-24))
    k = k * (1.0 + (a - 1.0) * ka_ref[...])
    bonus = _head_sum(r * k * rk_ref[...]) * v
    b = kk * a
    for p in range(NP):
        ps_ = slice(LANES * p, LANES * (p + 1))
        r_s[p] = r[:, ps_]
        w_s[p] = wdec[:, ps_]
        k_s[p] = k[:, ps_]
        b_s[p] = b[:, ps_]
        nkk_s[p] = -kk[:, ps_]
        v2 = v[:, ps_]
        vt_s[p] = jnp.concatenate([v2[:, 0:N], v2[:, N:2 * N]], axis=0).T
    ot_s[...] = jnp.zeros(ot_s.shape, F32)

    bd_r = lax.broadcasted_iota(jnp.int32, (2 * LANES, 2 * LANES), 0) // N
    bd_c = lax.broadcasted_iota(jnp.int32, (2 * LANES, 2 * LANES), 1) // N
    seg_ones = jnp.where(bd_r == bd_c, 1.0, 0.0).astype(BF16)

    def seg_sum(parts):
        half = len(parts) // 2
        outs = []
        for grp in (parts[:half], parts[half:]):
            x = jnp.concatenate([jnp.concatenate(grp[0::2], axis=0), jnp.concatenate(grp[1::2], axis=0)], axis=1)
            y = jnp.dot(x.astype(BF16), seg_ones, preferred_element_type=F32)
            for i in range(len(grp)):
                outs.append(y[N * (i // 2):N * (i // 2 + 1), LANES * (i % 2):LANES * (i % 2 + 1)])
        return outs

    lane_time = lax.broadcasted_iota(jnp.int32, (N, LANES), 1) % L
    row = lambda ref, p, t: ref[p, pl.ds(t, 1), :]

    def step(t, carry):
        tp = jnp.maximum(t - 1, 0)
        parts = []
        for p in range(NP):
            s = st_ref[p]
            parts += [s * row(nkk_s, p, t), jnp.where(lane_time == t, vt_s[p], 0.0), s * row(r_s, p, tp)]
        res = seg_sum(parts)
        for p in range(NP):
            sa, vcol, o_prev = res[3 * p:3 * p + 3]
            st_ref[p] = st_ref[p] * row(w_s, p, t) + sa * row(b_s, p, t) + vcol * row(k_s, p, t)
            ot_s[p] = jnp.where(lane_time == t - 1, o_prev, ot_s[p])
        return carry

    lax.fori_loop(0, L, step, 0, unroll=2)
    o_last = seg_sum([st_ref[p] * row(r_s, p, L - 1) for p in range(NP)])
    for p in range(NP):
        ot_s[p] = jnp.where(lane_time == L - 1, o_last[p], ot_s[p])

    for p in range(NP):
        ot = ot_s[p]
        mean = jnp.mean(ot, axis=0, keepdims=True)
        cen = ot - mean
        var = jnp.mean(cen * cen, axis=0, keepdims=True)
        on_t = (cen * lax.rsqrt(var + RW_LN_EPS)).T
        o_s[:, LANES * p:LANES * p + N] = on_t[0:L]
        o_s[:, LANES * p + N:LANES * (p + 1)] = on_t[L:2 * L]
    o = o_s[...] * lng_ref[...] + lnb_ref[...]
    y_ref[0] = (o + bonus) * g

    @pl.when(c == pl.num_programs(1) - 1)
    def _():
        shifto_ref[0] = ext_ref[0:SUBLANES, :]
        sto_ref[0] = st_ref[...]


def rwkv_mixer(proj, shift_state, rw_state, mu, w0, w2, a0, a2, g2, k_k, k_a, r_k, ln_g, ln_b):
    B, T, _ = proj.shape
    L = RW_L
    assert T % L == 0
    H, N = RW_HEADS, RW_N
    shift0 = jnp.pad(shift_state[:, None, :], ((0, 0), (SUBLANES - 1, 0), (0, 0)))
    w2p = jnp.pad(w2, ((0, LANES - w2.shape[0]), (0, 0)))
    a2p = jnp.pad(a2, ((LANES - a2.shape[0], 0), (0, 0)))
    vec = lambda w: pl.BlockSpec((1, w), lambda b, c: (0, 0))
    mat = pl.BlockSpec((LANES, D), lambda b, c: (0, 0))
    st_spec = pl.BlockSpec((1, H // 2, N, 2 * N), lambda b, c: (b, 0, 0, 0))
    sh_spec = pl.BlockSpec((1, SUBLANES, RW_COLS), lambda b, c: (b, 0, 0))
    rows = pltpu.VMEM((H // 2, L, 2 * N), F32)
    cols = pltpu.VMEM((H // 2, N, 2 * L), F32)
    st0 = jnp.swapaxes(rw_state.reshape(B, H // 2, 2, N, N), 2, 3).reshape(B, H // 2, N, 2 * N)
    y, shifto, sto = pl.pallas_call(
        _rwkv_kernel,
        grid=(B, T // L),
        in_specs=[pl.BlockSpec((1, L, 3 * D), lambda b, c: (b, c, 0)),
                  pl.BlockSpec((1, L, RW_LORA), lambda b, c: (b, c, ODD_LORA_BLOCK)), sh_spec, st_spec,
                  vec(RW_COLS), vec(D), mat, vec(D), mat, mat, vec(D), vec(D), vec(D), vec(D), vec(D)],
        out_specs=[pl.BlockSpec((1, L, D), lambda b, c: (b, c, 0)), sh_spec, st_spec],
        out_shape=[jax.ShapeDtypeStruct((B, T, D), F32),
                   jax.ShapeDtypeStruct((B, SUBLANES, RW_COLS), F32),
                   jax.ShapeDtypeStruct((B, H // 2, N, 2 * N), F32)],
        scratch_shapes=[pltpu.VMEM((SUBLANES + L, RW_COLS), F32), pltpu.VMEM((H // 2, N, 2 * N), F32),
                        rows, rows, rows, rows, rows, cols, cols, pltpu.VMEM((L, D), F32)],
        compiler_params=_cparams(("arbitrary", "arbitrary"), 48),
        name="rwkv_mixer",
    )(proj, proj, shift0, st0, mu.reshape(1, RW_COLS), w0.reshape(1, D), w2p, a0.reshape(1, D), a2p, g2,
      k_k.reshape(1, D), k_a.reshape(1, D), r_k.reshape(1, D), ln_g.reshape(1, D), ln_b.reshape(1, D))
    new_st = jnp.swapaxes(sto.reshape(B, H // 2, N, 2, N), 2, 3).reshape(B, H, N, N)
    return y, shifto[:, SUBLANES - 1], new_st


def _logf_kernel(fl_ref, bias_ref, o_ref):
    x = fl_ref[0] + bias_ref[...]
    lf = jnp.minimum(x, 0.0) - jnp.log1p(jnp.exp(-jnp.abs(x)))
    o_ref[0] = lf[:, 0:FOX_HEADS]


def fox_logf(proj, f_bias):
    B, T, _ = proj.shape
    tt = min(T, 512)
    return pl.pallas_call(
        _logf_kernel,
        grid=(B, T // tt),
        in_specs=[pl.BlockSpec((1, tt, LANES), lambda b, i: (b, i, ODD_FL_BLOCK)),
                  pl.BlockSpec((1, LANES), lambda b, i: (0, 0))],
        out_specs=pl.BlockSpec((1, tt, FOX_HEADS), lambda b, i: (b, i, 0)),
        out_shape=jax.ShapeDtypeStruct((B, T, FOX_HEADS), F32),
        compiler_params=_cparams(("arbitrary", "arbitrary")),
        name="fox_logf",
    )(proj, jnp.pad(f_bias.reshape(1, FOX_HEADS), ((0, 0), (0, LANES - FOX_HEADS))))


def _cumsum_kernel(x_ref, o_ref, carry_ref):
    ts = x_ref.shape[1]

    @pl.when(pl.program_id(1) == 0)
    def _():
        carry_ref[...] = jnp.zeros(carry_ref.shape, F32)

    row = lax.broadcasted_iota(jnp.int32, (ts, ts), 0)
    col = lax.broadcasted_iota(jnp.int32, (ts, ts), 1)
    cum = _mm_hi((col <= row).astype(F32), x_ref[0]) + carry_ref[...]
    o_ref[0] = cum
    carry_ref[...] = cum[ts - 1:ts, :]


def time_cumsum(x):
    B, S, H = x.shape
    ts = 512 if S % 512 == 0 else LANES
    assert S % ts == 0
    return pl.pallas_call(
        _cumsum_kernel,
        grid=(B, S // ts),
        in_specs=[pl.BlockSpec((1, ts, H), lambda b, i: (b, i, 0))],
        out_specs=pl.BlockSpec((1, ts, H), lambda b, i: (b, i, 0)),
        out_shape=jax.ShapeDtypeStruct((B, S, H), F32),
        scratch_shapes=[pltpu.VMEM((1, H), F32)],
        compiler_params=_cparams(("arbitrary", "arbitrary")),
        name="time_cumsum",
    )(x)


def _aug_placement(mode):
    h = jnp.arange(LANES)[:, None]
    c = jnp.arange(FOX_AW)[None, :]
    base = h * LANES + FOX_HD
    valid = h < FOX_HEADS
    term_off = 0 if mode == 'q' else 3
    sign = 1.0 if mode == 'q' else -1.0
    place = jnp.stack([jnp.where(valid & (c == base + term_off + i), sign, 0.0) for i in range(3)]).astype(BF16)
    lane = jnp.arange(FOX_AW) % LANES
    if mode == 'q':
        ones = (lane >= FOX_HD + 3) & (lane < FOX_HD + 6)
    elif mode == 'k':
        ones = (lane >= FOX_HD) & (lane < FOX_HD + 3)
    else:
        ones = lane == FOX_HD
    return place, jnp.where(ones, 1.0, 0.0).reshape(1, FOX_AW).astype(F32)


def _fox_aug_kernel(x_ref, cum_ref, place_ref, ones_ref, o_ref, *, mode):
    tt = x_ref.shape[1]
    x = x_ref[0]
    if mode == 'q':
        x = x * (FOX_HD ** -0.5 * LOG2E)
    aug = ones_ref[...]
    if mode != 'v':
        c = cum_ref[0] * LOG2E
        c1 = c.astype(BF16)
        r1 = c - c1.astype(F32)
        c2 = r1.astype(BF16)
        c3 = (r1 - c2.astype(F32)).astype(BF16)
        aug = aug + (jnp.dot(c1, place_ref[0], preferred_element_type=F32)
                     + jnp.dot(c2, place_ref[1], preferred_element_type=F32)
                     + jnp.dot(c3, place_ref[2], preferred_element_type=F32))
    else:
        aug = jnp.broadcast_to(aug, (tt, FOX_AW))
    lo = lax.broadcasted_iota(jnp.int32, (tt, LANES), 1) < FOX_HD
    for m in range(FOX_HEADS // 2):
        x_m = x[:, LANES * m:LANES * (m + 1)]
        even = jnp.where(lo, x_m, aug[:, 2 * LANES * m:2 * LANES * m + LANES])
        odd = jnp.where(lo, pltpu.roll(x_m, FOX_HD, 1), aug[:, 2 * LANES * m + LANES:2 * LANES * (m + 1)])
        o_ref[0, :, 2 * LANES * m:2 * LANES * (m + 1)] = jnp.concatenate([even, odd], axis=1).astype(BF16)


def fox_augment(arr, col_block, cum, mode):
    B, S, _ = arr.shape
    tt = min(S, 256)
    if S % tt:
        tt = LANES
    place, ones = _aug_placement(mode)
    out_spec = pl.BlockSpec((1, tt, FOX_AW), lambda b, i: (b, i, 0))
    out_shape = jax.ShapeDtypeStruct((B, S, FOX_AW), BF16)
    return pl.pallas_call(
        functools.partial(_fox_aug_kernel, mode=mode),
        grid=(B, S // tt),
        in_specs=[pl.BlockSpec((1, tt, D), lambda b, i: (b, i, col_block)),
                  pl.BlockSpec((1, tt, LANES), lambda b, i: (b, i, 0)),
                  pl.BlockSpec((3, LANES, FOX_AW), lambda b, i: (0, 0, 0)),
                  pl.BlockSpec((1, FOX_AW), lambda b, i: (0, 0))],
        out_specs=out_spec,
        out_shape=out_shape,
        compiler_params=_cparams(("arbitrary", "arbitrary")),
        name="fox_augment_" + mode,
    )(arr, cum, place, ones)


def _fox_kernel(i_ref, j_ref, qa_ref, ka_ref, va_ref, o_ref, m_ref, acc_ref, *, past, tq, tk):
    s_idx = pl.program_id(2)
    i = i_ref[s_idx]
    j = j_ref[s_idx]
    first_q = past + i * tq

    @pl.when(j == 0)
    def _():
        m_ref[...] = jnp.full(m_ref.shape, NEG_INF, F32)
        acc_ref[...] = jnp.zeros(acc_ref.shape, F32)

    def tile(masked):
        scores = []
        for hh in range(2):
            hs = slice(LANES * hh, LANES * (hh + 1))
            s = lax.dot_general(qa_ref[0, :, hs], ka_ref[0, :, hs], (((1,), (1,)), ((), ())),
                                preferred_element_type=F32)
            if masked:
                diff = (lax.broadcasted_iota(jnp.int32, (tq, tk), 1)
                        - lax.broadcasted_iota(jnp.int32, (tq, tk), 0))
                s = jnp.where(diff <= first_q - j * tk, s, NEG_INF)
            scores.append(s)
        for hh, s in enumerate(scores):
            hs = slice(LANES * hh, LANES * (hh + 1))
            chunks = [s[:, LANES * c:LANES * (c + 1)] for c in range(tk // LANES)]
            smax = functools.reduce(jnp.maximum, chunks)
            m_old = m_ref[hh]
            m_new = jnp.maximum(m_old, jnp.max(smax, axis=1, keepdims=True))
            pm = jnp.concatenate([jnp.exp2(ch - m_new) for ch in chunks], axis=1)
            acc_ref[hh] = (jnp.exp2(m_old - m_new) * acc_ref[hh]
                           + jnp.dot(pm.astype(BF16), va_ref[0, :, hs], preferred_element_type=F32))
            m_ref[hh] = m_new

    unmasked = (j + 1) * tk - 1 <= first_q

    @pl.when(unmasked)
    def _():
        tile(False)

    @pl.when(jnp.logical_not(unmasked))
    def _():
        tile(True)

    @pl.when(j == (first_q + tq - 1) // tk)
    def _():
        lo = lax.broadcasted_iota(jnp.int32, (tq, LANES), 1) < FOX_HD
        a0 = acc_ref[0]
        a1 = acc_ref[1]
        o0 = a0 / a0[:, FOX_HD:FOX_HD + 1]
        o1 = a1 / a1[:, FOX_HD:FOX_HD + 1]
        o_ref[0] = jnp.where(lo, o0, pltpu.roll(o1, FOX_HD, 1))


def fox_attention(qa, ka, va, past, T):
    B, S, _ = ka.shape
    tq = min(T, 512)
    tk = FOX_TK if S % FOX_TK == 0 else S
    nq = T // tq
    pairs = [(i, j) for i in range(nq) for j in range((past + (i + 1) * tq - 1) // tk + 1)]
    i_idx = jnp.asarray([p[0] for p in pairs], jnp.int32)
    j_idx = jnp.asarray([p[1] for p in pairs], jnp.int32)
    grid_spec = pltpu.PrefetchScalarGridSpec(
        num_scalar_prefetch=2,
        grid=(B, FOX_HEADS // 2, len(pairs)),
        in_specs=[
            pl.BlockSpec((1, tq, 2 * LANES), lambda b, p, s, ii, jj: (b, ii[s], p)),
            pl.BlockSpec((1, tk, 2 * LANES), lambda b, p, s, ii, jj: (b, jj[s], p)),
            pl.BlockSpec((1, tk, 2 * LANES), lambda b, p, s, ii, jj: (b, jj[s], p)),
        ],
        out_specs=pl.BlockSpec((1, tq, LANES), lambda b, p, s, ii, jj: (b, ii[s], p)),
        scratch_shapes=[pltpu.VMEM((2, tq, LANES), F32), pltpu.VMEM((2, tq, LANES), F32)],
    )
    return pl.pallas_call(
        functools.partial(_fox_kernel, past=past, tq=tq, tk=tk),
        grid_spec=grid_spec,
        out_shape=jax.ShapeDtypeStruct((B, T, D), F32),
        compiler_params=_cparams(("arbitrary", "arbitrary", "arbitrary"), 48),
        name="fox_attention",
    )(i_idx, j_idx, qa, ka, va)


def _trunk(x, mods, prm, wts, conv_st, ssd_st, hg_st, shift_st, rw_st, fk_past, fv_past, fl_past):
    B, T, _ = x.shape
    n = B * T
    sh1, sc1, gt1, sh2, sc2, gt2 = mods[0]
    proj = norm_inproj(x, prm['norm_mix_g'][0], sc1, sh1, wts['even_in'])
    ya, n_conv, n_ssd = ssd_mixer(proj, conv_st, ssd_st, prm['ssd_conv_w'][0], prm['ssd_conv_b'][0],
                                  prm['ssd_dt_bias'][0], prm['ssd_a_log'][0], prm['ssd_d'][0],
                                  prm['ssd_norm_g'][0])
    yb, n_hg = hgrn_mixer(proj, hg_st, wts['hgrn_lb'], prm['hgrn_norm_g'][0])
    x, hn = mix_outproj(ya, yb, wts['even_out'], x, gt1, prm['norm_ffn_g'][0], sc2, sh2)
    hn2 = hn.reshape(n, D)
    gates = moe_router(hn2, prm['moe_router_w'][0], prm['moe_router_bias'][0])
    ffn = moe_experts(hn2, gates, 0, prm['moe_w_gate'], prm['moe_w_up'], prm['moe_w_down'],
                      prm['moe_sh_gate'], prm['moe_sh_up'], prm['moe_sh_down'])
    x = residual(x, ffn, gt2, prm['final_norm_g'], False)

    sh1, sc1, gt1, sh2, sc2, gt2 = mods[1]
    proj = norm_inproj(x, prm['norm_mix_g'][1], sc1, sh1, wts['odd_in'])
    yc, n_shift, n_rw = rwkv_mixer(proj, shift_st, rw_st, prm['rwkv_mu'][0], prm['rwkv_w0'][0], prm['rwkv_w2'][0],
                                   prm['rwkv_a0'][0], prm['rwkv_a2'][0], prm['rwkv_g2'][0], prm['rwkv_k_k'][0],
                                   prm['rwkv_k_a'][0], prm['rwkv_r_k'][0], prm['rwkv_ln_g'][0],
                                   prm['rwkv_ln_b'][0])
    logf = fox_logf(proj, prm['fox_f_bias'][0])
    kf = proj[:, :, ODD_K_BLOCK * D:(ODD_K_BLOCK + 1) * D]
    vf = proj[:, :, ODD_V_BLOCK * D:(ODD_V_BLOCK + 1) * D]
    lane_pad = lambda a: jnp.pad(a, ((0, 0), (0, 0), (0, LANES - FOX_HEADS)))
    if fk_past is None:
        past = 0
        cum = time_cumsum(lane_pad(logf))
        ka = fox_augment(proj, ODD_K_BLOCK, cum, 'k')
        va = fox_augment(proj, ODD_V_BLOCK, cum, 'v')
    else:
        past = fk_past.shape[1]
        pad = (-(past + T)) % LANES
        lf_all = jnp.concatenate([fl_past, logf, jnp.zeros((B, pad, FOX_HEADS), F32)], axis=1)
        cum = time_cumsum(lane_pad(lf_all))
        k_arr = jnp.concatenate([fk_past.reshape(B, past, D), kf, jnp.zeros((B, pad, D), F32)], axis=1)
        v_arr = jnp.concatenate([fv_past.reshape(B, past, D), vf, jnp.zeros((B, pad, D), F32)], axis=1)
        ka = fox_augment(k_arr, 0, cum, 'k')
        va = fox_augment(v_arr, 0, cum, 'v')
    qa = fox_augment(proj, ODD_Q_BLOCK, cum[:, past:past + T], 'q')
    yd = fox_attention(qa, ka, va, past, T)
    x, hn = mix_outproj(yc, yd, wts['odd_out'], x, gt1, prm['norm_ffn_g'][1], sc2, sh2)
    hn2 = hn.reshape(n, D)
    gates = moe_router(hn2, prm['moe_router_w'][1], prm['moe_router_bias'][1])
    ffn = moe_experts(hn2, gates, 1, prm['moe_w_gate'], prm['moe_w_up'], prm['moe_w_down'],
                      prm['moe_sh_gate'], prm['moe_sh_up'], prm['moe_sh_down'])
    y = residual(x, ffn, gt2, prm['final_norm_g'], True)
    hs = (B, T, FOX_HEADS, FOX_HD)
    return (y, n_conv[None], n_ssd[None], n_hg[None], n_shift[None], n_rw[None],
            kf.reshape(hs)[None], vf.reshape(hs)[None], logf[None])


def _prep_weights(prm):
    ein = prm['even_in_w'][0]
    z, xbc, dt, hq, hf, hi, hgate = jnp.split(ein, [1024, 2304, 2320, 3344, 4368, 5392], axis=1)
    even_in = jnp.concatenate([z, hq, hf, hi, hgate, xbc, dt, jnp.zeros((D, LANES - SSD_HEADS), F32)], axis=1)
    rkv, lora, qkv, fl = jnp.split(prm['odd_in_w'][0], [3 * D, RW_COLS, RW_COLS + 3 * D], axis=1)
    odd_in = jnp.concatenate([rkv, qkv, lora, fl, jnp.zeros((D, LANES - FOX_HEADS), F32)], axis=1)
    lb = jnp.cumsum(jax.nn.softmax(prm['hgrn_lb_logits'].astype(F32), axis=0), axis=0)[0]
    return {
        'even_in': even_in.astype(BF16),
        'odd_in': odd_in.astype(BF16),
        'even_out': prm['even_out_w'][0].astype(BF16),
        'odd_out': prm['odd_out_w'][0].astype(BF16),
        'hgrn_lb': lb,
    }


def kernel(x_prompt, x_sample, c_prompt, c_sample, state_ssd_conv, state_ssd, state_hgrn, state_rwkv_shift, state_rwkv, cache_fox_k, cache_fox_v, cache_fox_logf, ada_w, ada_b, norm_mix_g, norm_ffn_g, final_norm_g, even_in_w, even_out_w, ssd_conv_w, ssd_conv_b, ssd_dt_bias, ssd_a_log, ssd_d, ssd_norm_g, hgrn_lb_logits, hgrn_norm_g, odd_in_w, odd_out_w, rwkv_mu, rwkv_w0, rwkv_w2, rwkv_a0, rwkv_a2, rwkv_g2, rwkv_k_k, rwkv_k_a, rwkv_r_k, rwkv_ln_g, rwkv_ln_b, fox_f_bias, moe_router_w, moe_router_bias, moe_w_gate, moe_w_up, moe_w_down, moe_sh_gate, moe_sh_up, moe_sh_down):
    prm = dict(ada_w=ada_w, ada_b=ada_b, norm_mix_g=norm_mix_g, norm_ffn_g=norm_ffn_g, final_norm_g=final_norm_g,
               even_in_w=even_in_w, even_out_w=even_out_w, ssd_conv_w=ssd_conv_w, ssd_conv_b=ssd_conv_b,
               ssd_dt_bias=ssd_dt_bias, ssd_a_log=ssd_a_log, ssd_d=ssd_d, ssd_norm_g=ssd_norm_g,
               hgrn_lb_logits=hgrn_lb_logits, hgrn_norm_g=hgrn_norm_g,
               odd_in_w=odd_in_w, odd_out_w=odd_out_w, rwkv_mu=rwkv_mu, rwkv_w0=rwkv_w0, rwkv_w2=rwkv_w2,
               rwkv_a0=rwkv_a0, rwkv_a2=rwkv_a2, rwkv_g2=rwkv_g2, rwkv_k_k=rwkv_k_k, rwkv_k_a=rwkv_k_a,
               rwkv_r_k=rwkv_r_k, rwkv_ln_g=rwkv_ln_g, rwkv_ln_b=rwkv_ln_b, fox_f_bias=fox_f_bias,
               moe_router_w=moe_router_w, moe_router_bias=moe_router_bias, moe_w_gate=moe_w_gate,
               moe_w_up=moe_w_up, moe_w_down=moe_w_down, moe_sh_gate=moe_sh_gate, moe_sh_up=moe_sh_up,
               moe_sh_down=moe_sh_down)
    wts = _prep_weights(prm)
    bp, bs = x_prompt.shape[0], x_sample.shape[0]
    depth = ada_w.shape[0]
    c_all = jnp.concatenate([c_prompt, c_sample], axis=0)
    rows = c_all.shape[0]
    c_all = jnp.pad(c_all, ((0, (-rows) % SUBLANES), (0, 0)))
    mod = ada_mod(c_all, ada_w, ada_b)

    def mods_for(lo, hi):
        return [tuple(m[:, None, :] for m in jnp.split(mod[l, lo:hi], 6, axis=-1)) for l in range(depth)]

    z = lambda *s: jnp.zeros(s, F32)
    out_p = _trunk(x_prompt, mods_for(0, bp), prm, wts,
                   z(bp, SSD_CONV - 1, SSD_CONV_DIM), z(bp, SSD_HEADS, SSD_HD, SSD_STATE),
                   z(bp, HG_HEADS, HG_DK, HG_DV), z(bp, RW_COLS), z(bp, RW_HEADS, RW_N, RW_N),
                   None, None, None)
    out_s = _trunk(x_sample, mods_for(bp, bp + bs), prm, wts,
                   state_ssd_conv[0], state_ssd[0], state_hgrn[0], state_rwkv_shift[0], state_rwkv[0],
                   cache_fox_k[0], cache_fox_v[0], cache_fox_logf[0])
    return (out_p[0], out_s[0]) + tuple(out_p[1:]) + tuple(out_s[1:])
```

```python
import functools

import jax
import jax.numpy as jnp
from jax import lax
from jax.experimental import pallas as pl
from jax.experimental.pallas import tpu as pltpu

F32 = jnp.float32
BF16 = jnp.bfloat16
HI = lax.Precision.HIGHEST

D = 1024
EPS = 1e-6
LANES = 128
SUBLANES = 8

SSD_HEADS = 16
SSD_HD = 64
SSD_STATE = 64
SSD_CONV = 4
SSD_CONV_DIM = 1280
SSD_L = 64
HG_HEADS = 8
HG_DK = 128
HG_DV = 128
HG_L = 64
RW_HEADS = 16
RW_N = 64
RW_COLS = 3328
RW_LN_EPS = 64e-5
RW_L = 64
FOX_HEADS = 16
FOX_HD = 64
FOX_AW = FOX_HEADS * LANES
FOX_TK = 1024
LOG2E = 1.4426950408889634
ODD_Q_BLOCK, ODD_K_BLOCK, ODD_V_BLOCK = 3, 4, 5
ODD_LORA_BLOCK = 24
ODD_FL_BLOCK = 50
RW_LORA = 256
N_EXPERTS = 64
TOP_K = 6
N_GROUPS = 8
TOPK_GROUPS = 4
D_EXPERT = 256
ROUTED_SCALE = 2.5
PROJ_COLS = 6528
PROJ_TN = 2176

NEG_INF = float("-inf")


def _cparams(sem, vmem_mb=None):
    kw = dict(dimension_semantics=sem)
    if vmem_mb is not None:
        kw["vmem_limit_bytes"] = vmem_mb * 1024 * 1024
    return pltpu.CompilerParams(**kw)


def _mm(a, b):
    return jnp.dot(a.astype(BF16), b.astype(BF16), preferred_element_type=F32)


def _mm_nt(a, b):
    return lax.dot_general(a.astype(BF16), b.astype(BF16), (((1,), (1,)), ((), ())),
                           preferred_element_type=F32)


def _mm_hi(a, b):
    return jnp.dot(a, b, precision=HI, preferred_element_type=F32)


def _mm_01(sel_bf, x):
    x1 = x.astype(BF16)
    r1 = x - x1.astype(F32)
    x2 = r1.astype(BF16)
    x3 = (r1 - x2.astype(F32)).astype(BF16)
    return (jnp.dot(sel_bf, x1, preferred_element_type=F32) + jnp.dot(sel_bf, x2, preferred_element_type=F32)
            + jnp.dot(sel_bf, x3, preferred_element_type=F32))


def _sigmoid(x):
    return 1.0 / (1.0 + jnp.exp(-x))


def _silu(x):
    return x * _sigmoid(x)


def _softplus(x):
    return jnp.maximum(x, 0.0) + jnp.log1p(jnp.exp(-jnp.abs(x)))


def _row_tiles(G, gs, target):
    if gs >= target:
        assert gs % target == 0
        return 1, target
    gb = max(1, min(G, target // gs))
    while G % gb:
        gb -= 1
    return gb, gs


def _ada_kernel(c_ref, w_ref, b_ref, o_ref):
    c = c_ref[...]
    o_ref[0] = _mm(_silu(c), w_ref[0]) + b_ref[0]


def ada_mod(c_all, ada_w, ada_b):
    depth = ada_w.shape[0]
    bp = c_all.shape[0]
    tn = 1024
    return pl.pallas_call(
        _ada_kernel,
        grid=(depth, 6 * D // tn),
        in_specs=[
            pl.BlockSpec((bp, D), lambda l, j: (0, 0)),
            pl.BlockSpec((1, D, tn), lambda l, j: (l, 0, j)),
            pl.BlockSpec((1, 1, tn), lambda l, j: (l, 0, j)),
        ],
        out_specs=pl.BlockSpec((1, bp, tn), lambda l, j: (l, 0, j)),
        out_shape=jax.ShapeDtypeStruct((depth, bp, 6 * D), F32),
        compiler_params=_cparams(("arbitrary", "arbitrary")),
        name="ada_mod",
    )(c_all, ada_w, ada_b.reshape(depth, 1, 6 * D))


def _rms_mod(x, g, sc, sh):
    y = x * lax.rsqrt(jnp.mean(x * x, axis=-1, keepdims=True) + EPS)
    return (y * g) * (1.0 + sc) + sh


def _inproj_kernel(x_ref, g_ref, sc_ref, sh_ref, w_ref, o_ref, xn_ref):
    @pl.when(pl.program_id(1) == 0)
    def _():
        hn = _rms_mod(x_ref[...], g_ref[...], sc_ref[...], sh_ref[...])
        xn_ref[...] = hn.reshape(xn_ref.shape).astype(BF16)

    o_ref[...] = jnp.dot(xn_ref[...], w_ref[...], preferred_element_type=F32)


def norm_inproj(x, g, sc, sh, w_bf):
    G, gs, _ = x.shape
    gb, tb = _row_tiles(G, gs, 512)
    nt = gs // tb
    tt = gb * tb
    n = G * gs
    out = pl.pallas_call(
        _inproj_kernel,
        grid=(n // tt, PROJ_COLS // PROJ_TN),
        in_specs=[
            pl.BlockSpec((gb, tb, D), lambda i, j: (i // nt, i % nt, 0)),
            pl.BlockSpec((1, D), lambda i, j: (0, 0)),
            pl.BlockSpec((gb, 1, D), lambda i, j: (i // nt, 0, 0)),
            pl.BlockSpec((gb, 1, D), lambda i, j: (i // nt, 0, 0)),
            pl.BlockSpec((D, PROJ_TN), lambda i, j: (0, j)),
        ],
        out_specs=pl.BlockSpec((tt, PROJ_TN), lambda i, j: (i, j)),
        out_shape=jax.ShapeDtypeStruct((n, PROJ_COLS), F32),
        scratch_shapes=[pltpu.VMEM((tt, D), BF16)],
        compiler_params=_cparams(("arbitrary", "arbitrary"), 48),
        name="norm_inproj",
    )(x, g.reshape(1, D), sc, sh, w_bf)
    return out.reshape(G, gs, PROJ_COLS)


def _outproj_kernel(ya_ref, yb_ref, w_ref, x_ref, gt_ref, g_ref, sc_ref, sh_ref, xo_ref, hn_ref):
    gb, tb, _ = x_ref.shape
    out = (jnp.dot(ya_ref[...].astype(BF16), w_ref[0:D, :], preferred_element_type=F32)
           + jnp.dot(yb_ref[...].astype(BF16), w_ref[D:2 * D, :], preferred_element_type=F32))
    x = x_ref[...] + gt_ref[...] * out.reshape(gb, tb, D)
    xo_ref[...] = x
    hn_ref[...] = _rms_mod(x, g_ref[...], sc_ref[...], sh_ref[...])


def mix_outproj(ya, yb, w_bf, x, gt, g, sc, sh):
    G, gs, _ = x.shape
    gb, tb = _row_tiles(G, gs, 512)
    nt = gs // tb
    tt = gb * tb
    n = G * gs
    row3 = pl.BlockSpec((gb, tb, D), lambda i: (i // nt, i % nt, 0))
    mod3 = pl.BlockSpec((gb, 1, D), lambda i: (i // nt, 0, 0))
    return pl.pallas_call(
        _outproj_kernel,
        grid=(n // tt,),
        in_specs=[
            pl.BlockSpec((tt, D), lambda i: (i, 0)),
            pl.BlockSpec((tt, D), lambda i: (i, 0)),
            pl.BlockSpec((2 * D, D), lambda i: (0, 0)),
            row3, mod3,
            pl.BlockSpec((1, D), lambda i: (0, 0)),
            mod3, mod3,
        ],
        out_specs=[row3, row3],
        out_shape=[jax.ShapeDtypeStruct((G, gs, D), F32), jax.ShapeDtypeStruct((G, gs, D), F32)],
        compiler_params=_cparams(("arbitrary",), 48),
        name="mix_outproj",
    )(ya.reshape(n, D), yb.reshape(n, D), w_bf, x, gt, g.reshape(1, D), sc, sh)


def _ssd_kernel(z_ref, xbc_ref, dt_ref, conv0_ref, st0_ref, cw_ref, cb_ref, dtb_ref, alog_ref, dfull_ref,
                ng_ref, y_ref, convo_ref, sto_ref, ext_ref, st_ref, yacc_ref):
    L = SSD_L
    c = pl.program_id(1)

    bd_r = lax.broadcasted_iota(jnp.int32, (LANES, LANES), 0) // SSD_HD
    bd_c = lax.broadcasted_iota(jnp.int32, (LANES, LANES), 1) // SSD_HD
    blockdiag = bd_r == bd_c

    @pl.when(c == 0)
    def _():
        ext_ref[0:SUBLANES, :] = conv0_ref[0]
        for p in range(SSD_HEADS // 2):
            x = st0_ref[0, p]
            st_ref[p] = jnp.where(blockdiag, jnp.concatenate([x, x], axis=1), 0.0)

    u = xbc_ref[0]
    ext_ref[SUBLANES:SUBLANES + L, :] = u
    acc = cb_ref[...] + u * cw_ref[SSD_CONV - 1:SSD_CONV, :]
    for i in range(SSD_CONV - 1):
        sh = SSD_CONV - 1 - i
        acc = acc + ext_ref[SUBLANES - sh:SUBLANES - sh + L, :] * cw_ref[i:i + 1, :]
    ext_ref[0:SUBLANES, :] = ext_ref[L:L + SUBLANES, :]
    xbc = _silu(acc)
    xs = xbc[:, 0:D]
    bm2 = xbc[:, D:D + LANES]
    cm2 = xbc[:, D + LANES:D + 2 * LANES]

    dt = _softplus(dt_ref[0] + dtb_ref[...])
    a = dt * (-jnp.exp(alog_ref[...]))
    row = lax.broadcasted_iota(jnp.int32, (L, L), 0)
    col = lax.broadcasted_iota(jnp.int32, (L, L), 1)
    tril = (col <= row).astype(F32)
    triu = (row <= col).astype(F32)
    cum = _mm_hi(tril, a)
    cum_t = _mm_hi(a.T, triu)
    eh = lax.broadcasted_iota(jnp.int32, (LANES, D), 0)
    ec = lax.broadcasted_iota(jnp.int32, (LANES, D), 1)
    expand = (ec // SSD_HD == eh).astype(F32)
    dt_full = _mm_hi(dt, expand)
    cum_full = _mm_hi(cum, expand)
    ecum_full = jnp.exp(cum_full)
    last_full = cum_full[L - 1:L, :]
    w_full = jnp.exp(last_full - cum_full)
    xdt = xs * dt_full

    lane = lax.broadcasted_iota(jnp.int32, (L, LANES), 1)
    lo = lane < SSD_HD
    tri2 = (lane % L) <= lax.broadcasted_iota(jnp.int32, (L, LANES), 0)
    bm_r = pltpu.roll(bm2, SSD_HD, 1)
    cm_r = pltpu.roll(cm2, SSD_HD, 1)
    bmstack = jnp.concatenate([bm2, bm2], axis=0)

    for g in range(2):
        grp_lo = lo if g == 0 else jnp.logical_not(lo)
        cm_m = jnp.where(grp_lo, cm2, 0.0)
        cb_dup = _mm_nt(cm_m, bmstack)
        cm_dup = jnp.where(grp_lo, cm2, cm_r)
        bm_dup = jnp.where(grp_lo, bm2, bm_r)
        for pp in range(4):
            p = 4 * g + pp
            sl = slice(LANES * p, LANES * (p + 1))
            cum_pair = cum_full[:, sl]
            cum_row = jnp.concatenate([cum_t[2 * p:2 * p + 1, :], cum_t[2 * p + 1:2 * p + 2, :]], axis=1)
            s2 = cb_dup * jnp.exp(jnp.where(tri2, cum_pair - cum_row, NEG_INF))
            x2 = xdt[:, sl]
            xstack = jnp.concatenate([jnp.where(lo, x2, 0.0), jnp.where(lo, 0.0, x2)], axis=0)
            hp = st_ref[p]
            lhs = jnp.concatenate([s2, cm_dup * ecum_full[:, sl]], axis=1)
            rhs = jnp.concatenate([xstack, hp], axis=0)
            yacc_ref[:, sl] = _mm(lhs, rhs)
            bw_t = (bm_dup * w_full[:, sl]).T
            upd = _mm(bw_t, x2)
            st_ref[p] = jnp.where(blockdiag, hp * ecum_full[L - 1:L, sl] + upd, 0.0)

    y = yacc_ref[...] + xs * dfull_ref[...]
    z = z_ref[0]
    yz = y * _silu(z)
    y_ref[0] = yz * lax.rsqrt(jnp.mean(yz * yz, axis=-1, keepdims=True) + EPS) * ng_ref[...]

    @pl.when(c == pl.num_programs(1) - 1)
    def _():
        convo_ref[0] = ext_ref[0:SUBLANES, :]
        first = lax.broadcasted_iota(jnp.int32, (LANES, SSD_HD), 0) < SSD_STATE
        for p in range(SSD_HEADS // 2):
            hp = st_ref[p]
            sto_ref[0, p] = jnp.where(first, hp[:, 0:SSD_HD], hp[:, SSD_HD:2 * SSD_HD])


def ssd_mixer(proj, conv_state, ssd_state, conv_w, conv_b, dt_bias, a_log, d_skip, norm_g):
    B, T, _ = proj.shape
    L = SSD_L
    assert T % L == 0
    H, P, N = SSD_HEADS, SSD_HD, SSD_STATE
    conv0 = jnp.pad(conv_state, ((0, 0), (SUBLANES - (SSD_CONV - 1), 0), (0, 0)))
    st0 = jnp.swapaxes(ssd_state, -1, -2).reshape(B, H // 2, 2 * N, P)
    st_spec = pl.BlockSpec((1, H // 2, 2 * N, P), lambda b, c: (b, 0, 0, 0))
    cw =jnp.pad(conv_w, ((0, SUBLANES - SSD_CONV), (0, 0)))
    pad16 = lambda v: jnp.pad(v.reshape(1, H), ((0, 0), (0, LANES - H)))
    dfull = jnp.repeat(d_skip, P).reshape(1, D)
    vec = lambda w: pl.BlockSpec((1, w), lambda b, c: (0, 0))
    y, convo, sto = pl.pallas_call(
        _ssd_kernel,
        grid=(B, T // L),
        in_specs=[
            pl.BlockSpec((1, L, D), lambda b, c: (b, c, 0)),
            pl.BlockSpec((1, L, SSD_CONV_DIM), lambda b, c: (b, c, 4)),
            pl.BlockSpec((1, L, LANES), lambda b, c: (b, c, 50)),
            pl.BlockSpec((1, SUBLANES, SSD_CONV_DIM), lambda b, c: (b, 0, 0)),
            st_spec,
            pl.BlockSpec((SUBLANES, SSD_CONV_DIM), lambda b, c: (0, 0)),
            vec(SSD_CONV_DIM), vec(LANES), vec(LANES), vec(D), vec(D),
        ],
        out_specs=[
            pl.BlockSpec((1, L, D), lambda b, c: (b, c, 0)),
            pl.BlockSpec((1, SUBLANES, SSD_CONV_DIM), lambda b, c: (b, 0, 0)),
            st_spec,
        ],
        out_shape=[
            jax.ShapeDtypeStruct((B, T, D), F32),
            jax.ShapeDtypeStruct((B, SUBLANES, SSD_CONV_DIM), F32),
            jax.ShapeDtypeStruct((B, H // 2, 2 * N, P), F32),
        ],
        scratch_shapes=[
            pltpu.VMEM((SUBLANES + L, SSD_CONV_DIM), F32),
            pltpu.VMEM((H // 2, LANES, LANES), F32),
            pltpu.VMEM((L, D), F32),
        ],
        compiler_params=_cparams(("arbitrary", "arbitrary"), 48),
        name="ssd_mixer",
    )(proj, proj, proj, conv0, st0, cw, conv_b.reshape(1, SSD_CONV_DIM), pad16(dt_bias), pad16(a_log),
      dfull, norm_g.reshape(1, D))
    return y, convo[:, SUBLANES - (SSD_CONV - 1):], jnp.swapaxes(sto.reshape(B, H, N, P), -1, -2)


def _hgrn_kernel(q_ref, f_ref, i_ref, gate_ref, st0_ref, lb_ref, ng_ref, y_ref, sto_ref, st_ref):
    L = HG_L
    c = pl.program_id(1)

    @pl.when(c == 0)
    def _():
        st_ref[...] = st0_ref[0]

    lb = lb_ref[...]
    f = lb + (1.0 - lb) * _sigmoid(f_ref[0])
    logf = jnp.log(f)
    q = _silu(q_ref[0]) * (HG_DK ** -0.5)
    k = 1.0 - f
    v = i_ref[0]
    row = lax.broadcasted_iota(jnp.int32, (L, L), 0)
    col = lax.broadcasted_iota(jnp.int32, (L, L), 1)
    cum = _mm_01(jnp.where(col <= row, 1.0, 0.0).astype(BF16), logf)
    qe = q * jnp.exp(cum)
    last = cum[L - 1:L, :]
    kw = k * jnp.exp(last - cum)
    elast = jnp.exp(last)

    halves = [L >> (i + 1) for i in range(L.bit_length() - 1)]
    small = [hs for hs in halves if 2 * hs <= SUBLANES]
    rsel = lax.broadcasted_iota(jnp.int32, (L * len(small), L), 0)
    csel = lax.broadcasted_iota(jnp.int32, (L * len(small), L), 1)
    ref_of = jnp.zeros_like(rsel)
    for i, hs in enumerate(small):
        ref_of = jnp.where(rsel // L == i, ((rsel - L * i) // (2 * hs)) * (2 * hs) + hs - 1, ref_of)
    ref_small = _mm_01(jnp.where(csel == ref_of, 1.0, 0.0).astype(BF16), cum)
    rvec = lax.broadcasted_iota(jnp.int32, (L, 1), 0)
    qs, ks, pair_mask = [], [], []
    for hs in halves:
        if hs in small:
            i = small.index(hs)
            cum_ref = ref_small[L * i:L * (i + 1)]
        else:
            cum_ref = jnp.concatenate(
                [jnp.broadcast_to(cum[b + hs - 1:b + hs, :], (2 * hs, D)) for b in range(0, L, 2 * hs)], axis=0)
        upper = (rvec % (2 * hs)) >= hs
        d = cum - cum_ref
        e = jnp.exp(jnp.where(upper, d, -d))
        qs.append((q * e).astype(BF16))
        ks.append((k * e).astype(BF16))
        pair_mask.append(jnp.where((row // (2 * hs) == col // (2 * hs)) & (row % (2 * hs) >= hs)
                                   & (col % (2 * hs) < hs), 1.0, 0.0))
    eye = jnp.where(row == col, 1.0, 0.0)
    qb, kb = q.astype(BF16), k.astype(BF16)

    heads = [slice(HG_DK * h, HG_DK * (h + 1)) for h in range(HG_HEADS)]
    scores = []
    for sl in heads:
        s = eye * _mm_nt(qb[:, sl], kb[:, sl])
        for i in range(len(halves)):
            s = s + pair_mask[i] * _mm_nt(qs[i][:, sl], ks[i][:, sl])
        scores.append(s.astype(BF16))
    vb = v.astype(BF16)
    outs = []
    for h, sl in enumerate(heads):
        st = st_ref[h]
        outs.append(_mm_nt(qe[:, sl], st) + jnp.dot(scores[h], vb[:, sl], preferred_element_type=F32))
        st_ref[h] = st * elast[:, sl] + _mm(v[:, sl].T, kw[:, sl])
    for (h, sl), o in zip(enumerate(heads), outs):
        on = o * lax.rsqrt(jnp.mean(o * o, axis=-1, keepdims=True) + EPS) * ng_ref[:, sl]
        y_ref[0, :, sl] = on * _silu(gate_ref[0, :, sl])

    @pl.when(c == pl.num_programs(1) - 1)
    def _():
        sto_ref[0] = st_ref[...]


def hgrn_mixer(proj, hg_state, lb, norm_g):
    B, T, _ = proj.shape
    L = HG_L
    assert T % L == 0
    st0 = jnp.swapaxes(hg_state, -1, -2)
    col = lambda j: pl.BlockSpec((1, L, D), lambda b, c: (b, c, j))
    st_spec = pl.BlockSpec((1, HG_HEADS, HG_DV, HG_DK), lambda b, c: (b, 0, 0, 0))
    vec = pl.BlockSpec((1, D), lambda b, c: (0, 0))
    y, sto = pl.pallas_call(
        _hgrn_kernel,
        grid=(B, T // L),
        in_specs=[col(1), col(2), col(3), col(4), st_spec, vec, vec],
        out_specs=[pl.BlockSpec((1, L, D), lambda b, c: (b, c, 0)), st_spec],
        out_shape=[jax.ShapeDtypeStruct((B, T, D), F32),
                   jax.ShapeDtypeStruct((B, HG_HEADS, HG_DV, HG_DK), F32)],
        scratch_shapes=[pltpu.VMEM((HG_HEADS, HG_DV, HG_DK), F32)],
        compiler_params=_cparams(("arbitrary", "arbitrary")),
        name="hgrn_mixer",
    )(proj, proj, proj, proj, st0, lb.reshape(1, D), norm_g.reshape(1, D))
    return y, jnp.swapaxes(sto, -1, -2)


def _first_max_onehot(vals, iota, axis, size):
    m = jnp.max(vals, axis=axis, keepdims=True)
    idx = jnp.min(jnp.where(vals == m, iota, size), axis=axis, keepdims=True)
    return iota == idx, m


def _router_kernel(x_ref, wt_ref, bias_ref, o_ref):
    tt = x_ref.shape[0]
    per = N_EXPERTS // N_GROUPS
    logits = lax.dot_general(wt_ref[...], x_ref[...], (((1,), (1,)), ((), ())),
                             precision=HI, preferred_element_type=F32)
    scores = _sigmoid(logits)
    biased = scores + bias_ref[...]
    b3 = biased.reshape(N_GROUPS, per, tt)
    i3 = lax.broadcasted_iota(jnp.int32, (N_GROUPS, per, tt), 1)
    oh, m1 = _first_max_onehot(b3, i3, 1, per)
    m2 = jnp.max(jnp.where(oh, NEG_INF, b3), axis=1, keepdims=True)
    grp = (m1 + m2).reshape(N_GROUPS, tt)
    ig = lax.broadcasted_iota(jnp.int32, (N_GROUPS, tt), 0)
    gsel = jnp.zeros((N_GROUPS, tt), jnp.bool_)
    for _ in range(TOPK_GROUPS):
        oh, _m = _first_max_onehot(grp, ig, 0, N_GROUPS)
        gsel = jnp.logical_or(gsel, oh)
        grp = jnp.where(oh, NEG_INF, grp)
    gmask = jnp.where(gsel, 1.0, 0.0).reshape(N_GROUPS, 1, tt)
    masked = jnp.where(gmask > 0.0, b3, NEG_INF).reshape(N_EXPERTS, tt)
    ie = lax.broadcasted_iota(jnp.int32, (N_EXPERTS, tt), 0)
    esel = jnp.zeros((N_EXPERTS, tt), F32)
    for _ in range(TOP_K):
        oh, _m = _first_max_onehot(masked, ie, 0, N_EXPERTS)
        esel = jnp.where(oh, 1.0, esel)
        masked = jnp.where(oh, NEG_INF, masked)
    wsel = scores * esel
    gates = wsel / jnp.sum(wsel, axis=0, keepdims=True) * ROUTED_SCALE
    o_ref[...] = gates.T


def moe_router(hn2d, router_w, router_bias):
    n = hn2d.shape[0]
    tt = min(n, 512)
    return pl.pallas_call(
        _router_kernel,
        grid=(n // tt,),
        in_specs=[
            pl.BlockSpec((tt, D), lambda i: (i, 0)),
            pl.BlockSpec((N_EXPERTS, D), lambda i: (0, 0)),
            pl.BlockSpec((N_EXPERTS, 1), lambda i: (0, 0)),
        ],
        out_specs=pl.BlockSpec((tt, N_EXPERTS), lambda i: (i, 0)),
        out_shape=jax.ShapeDtypeStruct((n, N_EXPERTS), F32),
        compiler_params=_cparams(("arbitrary",)),
        name="moe_router",
    )(hn2d, router_w.T, router_bias.reshape(N_EXPERTS, 1))


def _ffn(xb, wg, wu, wd, gate_col):
    h = _silu(jnp.dot(xb, wg.astype(BF16), preferred_element_type=F32)) * jnp.dot(
        xb, wu.astype(BF16), preferred_element_type=F32)
    if gate_col is not None:
        h = h * gate_col
    return jnp.dot(h.astype(BF16), wd.astype(BF16), preferred_element_type=F32)


def _moe_kernel(x_ref, gates_ref, wg_ref, wu_ref, wd_ref, sg_ref, su_ref, sd_ref, o_ref, xb_ref, acc_ref):
    e = pl.program_id(1)

    @pl.when(e == 0)
    def _():
        xb_ref[...] = x_ref[...].astype(BF16)
        acc_ref[...] = _ffn(xb_ref[...], sg_ref[0], su_ref[0], sd_ref[0], None)

    gates = gates_ref[...]
    lane = lax.broadcasted_iota(jnp.int32, gates.shape, 1)
    gate_col = jnp.sum(jnp.where(lane == e, gates, 0.0), axis=1, keepdims=True)
    acc_ref[...] += _ffn(xb_ref[...], wg_ref[0, 0], wu_ref[0, 0], wd_ref[0, 0], gate_col)

    @pl.when(e == pl.num_programs(1) - 1)
    def _():
        o_ref[...] = acc_ref[...]


def moe_experts(hn2d, gates, l, w_gate, w_up, w_down, sh_gate, sh_up, sh_down):
    n = hn2d.shape[0]
    tt = min(n, 1024)
    return pl.pallas_call(
        _moe_kernel,
        grid=(n // tt, N_EXPERTS),
        in_specs=[
            pl.BlockSpec((tt, D), lambda i, e: (i, 0)),
            pl.BlockSpec((tt, N_EXPERTS), lambda i, e: (i, 0)),
            pl.BlockSpec((1, 1, D, D_EXPERT), lambda i, e: (l, e, 0, 0)),
            pl.BlockSpec((1, 1, D, D_EXPERT), lambda i, e: (l, e, 0, 0)),
            pl.BlockSpec((1, 1, D_EXPERT, D), lambda i, e: (l, e, 0, 0)),
            pl.BlockSpec((1, D, D_EXPERT), lambda i, e: (l, 0, 0)),
            pl.BlockSpec((1, D, D_EXPERT), lambda i, e: (l, 0, 0)),
            pl.BlockSpec((1, D_EXPERT, D), lambda i, e: (l, 0, 0)),
        ],
        out_specs=pl.BlockSpec((tt, D), lambda i, e: (i, 0)),
        out_shape=jax.ShapeDtypeStruct((n, D), F32),
        scratch_shapes=[pltpu.VMEM((tt, D), BF16), pltpu.VMEM((tt, D), F32)],
        compiler_params=_cparams(("arbitrary", "arbitrary"), 56),
        name="moe_experts",
    )(hn2d, gates, w_gate, w_up, w_down, sh_gate, sh_up, sh_down)


def _residual_kernel(x_ref, f_ref, gt_ref, g_ref, o_ref, *, final_norm):
    x = x_ref[...] + gt_ref[...] * f_ref[...]
    if final_norm:
        x = x * lax.rsqrt(jnp.mean(x * x, axis=-1, keepdims=True) + EPS) * g_ref[...]
    o_ref[...] = x


def residual(x, ffn, gt, g, final_norm):
    G, gs, _ = x.shape
    gb, tb = _row_tiles(G, gs, 512)
    nt = gs // tb
    row3 = pl.BlockSpec((gb, tb, D), lambda i: (i // nt, i % nt, 0))
    return pl.pallas_call(
        functools.partial(_residual_kernel, final_norm=final_norm),
        grid=(G * gs // (gb * tb),),
        in_specs=[row3, row3, pl.BlockSpec((gb, 1, D), lambda i: (i // nt, 0, 0)),
                  pl.BlockSpec((1, D), lambda i: (0, 0))],
        out_specs=row3,
        out_shape=jax.ShapeDtypeStruct((G, gs, D), F32),
        compiler_params=_cparams(("arbitrary",)),
        name="residual",
    )(x, ffn.reshape(G, gs, D), gt, g.reshape(1, D))


def _head_sum(x):
    L = x.shape[0]
    lo = lax.broadcasted_iota(jnp.int32, (L, LANES), 1) < RW_N
    outs = []
    for p in range(RW_HEADS // 2):
        x2 = x[:, LANES * p:LANES * (p + 1)]
        s_lo = jnp.sum(jnp.where(lo, x2, 0.0), axis=1, keepdims=True)
        s_hi = jnp.sum(jnp.where(lo, 0.0, x2), axis=1, keepdims=True)
        outs.append(jnp.where(lo, s_lo, s_hi))
    return jnp.concatenate(outs, axis=1)


def _rwkv_kernel(rkv_ref, lora_ref, shift0_ref, st0_ref, mu_ref, w0_ref, w2_ref, a0_ref, a2_ref, g2_ref, kk_ref, ka_ref,
                 rk_ref, lng_ref, lnb_ref, y_ref, shifto_ref, sto_ref,
                 ext_ref, st_ref, r_s, w_s, k_s, b_s, nkk_s, vt_s, ot_s, o_s):
    L = RW_L
    N = RW_N
    NP = RW_HEADS // 2
    c = pl.program_id(1)

    @pl.when(c == 0)
    def _():
        ext_ref[0:SUBLANES, :] = shift0_ref[0]
        for p in range(NP):
            st_ref[p] = jnp.concatenate([st0_ref[0, 2 * p], st0_ref[0, 2 * p + 1]], axis=1)

    pr = jnp.concatenate([rkv_ref[0], lora_ref[0]], axis=1)
    ext_ref[SUBLANES:SUBLANES + L, :] = pr
    prev = ext_ref[SUBLANES - 1:SUBLANES - 1 + L, :]
    ext_ref[0:SUBLANES, :] = ext_ref[L:L + SUBLANES, :]
    ps = pr + (prev - pr) * mu_ref[...]
    r = ps[:, 0:D]
    k = ps[:, D:2 * D]
    v = ps[:, 2 * D:3 * D]
    wa = ps[:, 3 * D:3 * D + LANES]
    gd = ps[:, 3 * D + LANES:3 * D + 2 * LANES]
    w = -_softplus(-(w0_ref[...] + _mm(jnp.tanh(wa), w2_ref[...]))) - 0.5
    wdec = jnp.exp(-jnp.exp(w))
    a = _sigmoid(a0_ref[...] + _mm(wa, a2_ref[...]))
    g = _mm(_sigmoid(gd), g2_ref[...])
    kk = k * kk_ref[...]
    kk = kk * lax.rsqrt(jnp.maximum(_head_sum(kk * kk), 1e-24))
    k = k * (1.0 + (a - 1.0) * ka_ref[...])
    bonus = _head_sum(r * k * rk_ref[...]) * v
    b = kk * a
    for p in range(NP):
        ps_ = slice(LANES * p, LANES * (p + 1))
        r_s[p] = r[:, ps_]
        w_s[p] = wdec[:, ps_]
        k_s[p] = k[:, ps_]
        b_s[p] = b[:, ps_]
        nkk_s[p] = -kk[:, ps_]
        v2 = v[:, ps_]
        vt_s[p] = jnp.concatenate([v2[:, 0:N], v2[:, N:2 * N]], axis=0).T
    ot_s[...] = jnp.zeros(ot_s.shape, F32)

    bd_r = lax.broadcasted_iota(jnp.int32, (2 * LANES, 2 * LANES), 0) // N
    bd_c = lax.broadcasted_iota(jnp.int32, (2 * LANES, 2 * LANES), 1) // N
    seg_ones = jnp.where(bd_r == bd_c, 1.0, 0.0).astype(BF16)

    def seg_sum(parts):
        half = len(parts) // 2
        outs = []
        for grp in (parts[:half], parts[half:]):
            x = jnp.concatenate([jnp.concatenate(grp[0::2], axis=0), jnp.concatenate(grp[1::2], axis=0)], axis=1)
            y = jnp.dot(x.astype(BF16), seg_ones, preferred_element_type=F32)
            for i in range(len(grp)):
                outs.append(y[N * (i // 2):N * (i // 2 + 1), LANES * (i % 2):LANES * (i % 2 + 1)])
        return outs

    lane_time = lax.broadcasted_iota(jnp.int32, (N, LANES), 1) % L
    row = lambda ref, p, t: ref[p, pl.ds(t, 1), :]

    def step(t, carry):
        tp = jnp.maximum(t - 1, 0)
        parts = []
        for p in range(NP):
            s = st_ref[p]
            parts += [s * row(nkk_s, p, t), jnp.where(lane_time == t, vt_s[p], 0.0), s * row(r_s, p, tp)]
        res = seg_sum(parts)
        for p in range(NP):
            sa, vcol, o_prev = res[3 * p:3 * p + 3]
            st_ref[p] = st_ref[p] * row(w_s, p, t) + sa * row(b_s, p, t) + vcol * row(k_s, p, t)
            ot_s[p] = jnp.where(lane_time == t - 1, o_prev, ot_s[p])
        return carry

    lax.fori_loop(0, L, step, 0, unroll=2)
    o_last = seg_sum([st_ref[p] * row(r_s, p, L - 1) for p in range(NP)])
    for p in range(NP):
        ot_s[p] = jnp.where(lane_time == L - 1, o_last[p], ot_s[p])

    for p in range(NP):
        ot = ot_s[p]
        mean = jnp.mean(ot, axis=0, keepdims=True)
        cen = ot - mean
        var = jnp.mean(cen * cen, axis=0, keepdims=True)
        on_t = (cen * lax.rsqrt(var + RW_LN_EPS)).T
        o_s[:, LANES * p:LANES * p + N] = on_t[0:L]
        o_s[:, LANES * p + N:LANES * (p + 1)] = on_t[L:2 * L]
    o = o_s[...] * lng_ref[...] + lnb_ref[...]
    y_ref[0] = (o + bonus) * g

    @pl.when(c == pl.num_programs(1) - 1)
    def _():
        shifto_ref[0] = ext_ref[0:SUBLANES, :]
        for p in range(NP):
            s = st_ref[p]
            sto_ref[0, 2 * p] = s[:, 0:N]
            sto_ref[0, 2 * p + 1] = s[:, N:2 * N]


def rwkv_mixer(proj, shift_state, rw_state, mu, w0, w2, a0, a2, g2, k_k, k_a, r_k, ln_g, ln_b):
    B, T, _ = proj.shape
    L = RW_L
    assert T % L == 0
    H, N = RW_HEADS, RW_N
    shift0 = jnp.pad(shift_state[:, None, :], ((0, 0), (SUBLANES - 1, 0), (0, 0)))
    w2p = jnp.pad(w2, ((0, LANES - w2.shape[0]), (0, 0)))
    a2p = jnp.pad(a2, ((LANES - a2.shape[0], 0), (0, 0)))
    vec = lambda w: pl.BlockSpec((1, w), lambda b, c: (0, 0))
    mat = pl.BlockSpec((LANES, D), lambda b, c: (0, 0))
    st_spec = pl.BlockSpec((1, H, N, N), lambda b, c: (b, 0, 0, 0))
    sh_spec = pl.BlockSpec((1, SUBLANES, RW_COLS), lambda b, c: (b, 0, 0))
    rows = pltpu.VMEM((H // 2, L, 2 * N), F32)
    cols = pltpu.VMEM((H // 2, N, 2 * L), F32)
    y, shifto, sto = pl.pallas_call(
        _rwkv_kernel,
        grid=(B, T // L),
        in_specs=[pl.BlockSpec((1, L, 3 * D), lambda b, c: (b, c, 0)),
                  pl.BlockSpec((1, L, RW_LORA), lambda b, c: (b, c, ODD_LORA_BLOCK)), sh_spec, st_spec,
                  vec(RW_COLS), vec(D), mat, vec(D), mat, mat, vec(D), vec(D), vec(D), vec(D), vec(D)],
        out_specs=[pl.BlockSpec((1, L, D), lambda b, c: (b, c, 0)), sh_spec, st_spec],
        out_shape=[jax.ShapeDtypeStruct((B, T, D), F32),
                   jax.ShapeDtypeStruct((B, SUBLANES, RW_COLS), F32),
                   jax.ShapeDtypeStruct((B, H, N, N), F32)],
        scratch_shapes=[pltpu.VMEM((SUBLANES + L, RW_COLS), F32), pltpu.VMEM((H // 2, N, 2 * N), F32),
                        rows, rows, rows, rows, rows, cols, cols, pltpu.VMEM((L, D), F32)],
        compiler_params=_cparams(("arbitrary", "arbitrary"), 48),
        name="rwkv_mixer",
    )(proj, proj, shift0, rw_state, mu.reshape(1, RW_COLS), w0.reshape(1, D), w2p, a0.reshape(1, D), a2p, g2,
      k_k.reshape(1, D), k_a.reshape(1, D), r_k.reshape(1, D), ln_g.reshape(1, D), ln_b.reshape(1, D))
    return y, shifto[:, SUBLANES - 1], sto


def _logf_kernel(fl_ref, bias_ref, o_ref):
    x = fl_ref[0] + bias_ref[...]
    lf = jnp.minimum(x, 0.0) - jnp.log1p(jnp.exp(-jnp.abs(x)))
    o_ref[0] = lf[:, 0:FOX_HEADS]


def fox_logf(proj, f_bias):
    B, T, _ = proj.shape
    tt = min(T, 512)
    return pl.pallas_call(
        _logf_kernel,
        grid=(B, T // tt),
        in_specs=[pl.BlockSpec((1, tt, LANES), lambda b, i: (b, i, ODD_FL_BLOCK)),
                  pl.BlockSpec((1, LANES), lambda b, i: (0, 0))],
        out_specs=pl.BlockSpec((1, tt, FOX_HEADS), lambda b, i: (b, i, 0)),
        out_shape=jax.ShapeDtypeStruct((B, T, FOX_HEADS), F32),
        compiler_params=_cparams(("arbitrary", "arbitrary")),
        name="fox_logf",
    )(proj, jnp.pad(f_bias.reshape(1, FOX_HEADS), ((0, 0), (0, LANES - FOX_HEADS))))


def _cumsum_kernel(x_ref, o_ref, carry_ref):
    ts = x_ref.shape[1]

    @pl.when(pl.program_id(1) == 0)
    def _():
        carry_ref[...] = jnp.zeros(carry_ref.shape, F32)

    row = lax.broadcasted_iota(jnp.int32, (ts, ts), 0)
    col = lax.broadcasted_iota(jnp.int32, (ts, ts), 1)
    cum = _mm_hi((col <= row).astype(F32), x_ref[0]) + carry_ref[...]
    o_ref[0] = cum
    carry_ref[...] = cum[ts - 1:ts, :]


def time_cumsum(x):
    B, S, H = x.shape
    ts = 512 if S % 512 == 0 else LANES
    assert S % ts == 0
    return pl.pallas_call(
        _cumsum_kernel,
        grid=(B, S // ts),
        in_specs=[pl.BlockSpec((1, ts, H), lambda b, i: (b, i, 0))],
        out_specs=pl.BlockSpec((1, ts, H), lambda b, i: (b, i, 0)),
        out_shape=jax.ShapeDtypeStruct((B, S, H), F32),
        scratch_shapes=[pltpu.VMEM((1, H), F32)],
        compiler_params=_cparams(("arbitrary", "arbitrary")),
        name="time_cumsum",
    )(x)


def _aug_placement(mode):
    h = jnp.arange(LANES)[:, None]
    c = jnp.arange(FOX_AW)[None, :]
    base = h * LANES + FOX_HD
    valid = h < FOX_HEADS
    term_off = 0 if mode == 'q' else 3
    sign = 1.0 if mode == 'q' else -1.0
    place = jnp.stack([jnp.where(valid & (c == base + term_off + i), sign, 0.0) for i in range(3)]).astype(BF16)
    lane = jnp.arange(FOX_AW) % LANES
    if mode == 'q':
        ones = (lane >= FOX_HD + 3) & (lane < FOX_HD + 6)
    elif mode == 'k':
        ones = (lane >= FOX_HD) & (lane < FOX_HD + 3)
    else:
        ones = lane == FOX_HD
    return place, jnp.where(ones, 1.0, 0.0).reshape(1, FOX_AW).astype(F32)


def _fox_aug_kernel(x_ref, cum_ref, place_ref, ones_ref, o_ref, *, mode):
    tt = x_ref.shape[1]
    x = x_ref[0]
    if mode == 'q':
        x = x * (FOX_HD ** -0.5 * LOG2E)
    aug = ones_ref[...]
    if mode != 'v':
        c = cum_ref[0] * LOG2E
        c1 = c.astype(BF16)
        r1 = c - c1.astype(F32)
        c2 = r1.astype(BF16)
        c3 = (r1 - c2.astype(F32)).astype(BF16)
        aug = aug + (jnp.dot(c1, place_ref[0], preferred_element_type=F32)
                     + jnp.dot(c2, place_ref[1], preferred_element_type=F32)
                     + jnp.dot(c3, place_ref[2], preferred_element_type=F32))
    else:
        aug = jnp.broadcast_to(aug, (tt, FOX_AW))
    lo = lax.broadcasted_iota(jnp.int32, (tt, LANES), 1) < FOX_HD
    for m in range(FOX_HEADS // 2):
        x_m = x[:, LANES * m:LANES * (m + 1)]
        even = jnp.where(lo, x_m, aug[:, 2 * LANES * m:2 * LANES * m + LANES])
        odd = jnp.where(lo, pltpu.roll(x_m, FOX_HD, 1), aug[:, 2 * LANES * m + LANES:2 * LANES * (m + 1)])
        o_ref[0, :, 2 * LANES * m:2 * LANES * (m + 1)] = jnp.concatenate([even, odd], axis=1).astype(BF16)


def fox_augment(arr, col_block, cum, mode):
    B, S, _ = arr.shape
    tt = min(S, 256)
    if S % tt:
        tt = LANES
    place, ones = _aug_placement(mode)
    out_spec = pl.BlockSpec((1, tt, FOX_AW), lambda b, i: (b, i, 0))
    out_shape = jax.ShapeDtypeStruct((B, S, FOX_AW), BF16)
    return pl.pallas_call(
        functools.partial(_fox_aug_kernel, mode=mode),
        grid=(B, S // tt),
        in_specs=[pl.BlockSpec((1, tt, D), lambda b, i: (b, i, col_block)),
                  pl.BlockSpec((1, tt, LANES), lambda b, i: (b, i, 0)),
                  pl.BlockSpec((3, LANES, FOX_AW), lambda b, i: (0, 0, 0)),
                  pl.BlockSpec((1, FOX_AW), lambda b, i: (0, 0))],
        out_specs=out_spec,
        out_shape=out_shape,
        compiler_params=_cparams(("arbitrary", "arbitrary")),
        name="fox_augment_" + mode,
    )(arr, cum, place, ones)


def _fox_kernel(i_ref, j_ref, qa_ref, ka_ref, va_ref, o_ref, m_ref, acc_ref, *, past, tq, tk):
    s_idx = pl.program_id(2)
    i = i_ref[s_idx]
    j = j_ref[s_idx]
    first_q = past + i * tq

    @pl.when(j == 0)
    def _():
        m_ref[...] = jnp.full(m_ref.shape, NEG_INF, F32)
        acc_ref[...] = jnp.zeros(acc_ref.shape, F32)

    def tile(masked):
        scores = []
        for hh in range(2):
            hs = slice(LANES * hh, LANES * (hh + 1))
            s = lax.dot_general(qa_ref[0, :, hs], ka_ref[0, :, hs], (((1,), (1,)), ((), ())),
                                preferred_element_type=F32)
            if masked:
                diff = (lax.broadcasted_iota(jnp.int32, (tq, tk), 1)
                        - lax.broadcasted_iota(jnp.int32, (tq, tk), 0))
                s = jnp.where(diff <= first_q - j * tk, s, NEG_INF)
            scores.append(s)
        for hh, s in enumerate(scores):
            hs = slice(LANES * hh, LANES * (hh + 1))
            chunks = [s[:, LANES * c:LANES * (c + 1)] for c in range(tk // LANES)]
            smax = functools.reduce(jnp.maximum, chunks)
            m_old = m_ref[hh]
            m_new = jnp.maximum(m_old, jnp.max(smax, axis=1, keepdims=True))
            pm = jnp.concatenate([jnp.exp2(ch - m_new) for ch in chunks], axis=1)
            acc_ref[hh] = (jnp.exp2(m_old - m_new) * acc_ref[hh]
                           + jnp.dot(pm.astype(BF16), va_ref[0, :, hs], preferred_element_type=F32))
            m_ref[hh] = m_new

    unmasked = (j + 1) * tk - 1 <= first_q

    @pl.when(unmasked)
    def _():
        tile(False)

    @pl.when(jnp.logical_not(unmasked))
    def _():
        tile(True)

    @pl.when(j == (first_q + tq - 1) // tk)
    def _():
        lo = lax.broadcasted_iota(jnp.int32, (tq, LANES), 1) < FOX_HD
        a0 = acc_ref[0]
        a1 = acc_ref[1]
        o0 = a0 / a0[:, FOX_HD:FOX_HD + 1]
        o1 = a1 / a1[:, FOX_HD:FOX_HD + 1]
        o_ref[0] = jnp.where(lo, o0, pltpu.roll(o1, FOX_HD, 1))


def fox_attention(qa, ka, va, past, T):
    B, S, _ = ka.shape
    tq = min(T, 512)
    tk = FOX_TK if S % FOX_TK == 0 else S
    nq = T // tq
    pairs = [(i, j) for i in range(nq) for j in range((past + (i + 1) * tq - 1) // tk + 1)]
    i_idx = jnp.asarray([p[0] for p in pairs], jnp.int32)
    j_idx = jnp.asarray([p[1] for p in pairs], jnp.int32)
    grid_spec = pltpu.PrefetchScalarGridSpec(
        num_scalar_prefetch=2,
        grid=(B, FOX_HEADS // 2, len(pairs)),
        in_specs=[
            pl.BlockSpec((1, tq, 2 * LANES), lambda b, p, s, ii, jj: (b, ii[s], p)),
            pl.BlockSpec((1, tk, 2 * LANES), lambda b, p, s, ii, jj: (b, jj[s], p)),
            pl.BlockSpec((1, tk, 2 * LANES), lambda b, p, s, ii, jj: (b, jj[s], p)),
        ],
        out_specs=pl.BlockSpec((1, tq, LANES), lambda b, p, s, ii, jj: (b, ii[s], p)),
        scratch_shapes=[pltpu.VMEM((2, tq, LANES), F32), pltpu.VMEM((2, tq, LANES), F32)],
    )
    return pl.pallas_call(
        functools.partial(_fox_kernel, past=past, tq=tq, tk=tk),
        grid_spec=grid_spec,
        out_shape=jax.ShapeDtypeStruct((B, T, D), F32),
        compiler_params=_cparams(("arbitrary", "arbitrary", "arbitrary"), 48),
        name="fox_attention",
    )(i_idx, j_idx, qa, ka, va)


def _trunk(x, mods, prm, wts, conv_st, ssd_st, hg_st, shift_st, rw_st, fk_past, fv_past, fl_past):
    B, T, _ = x.shape
    n = B * T
    sh1, sc1, gt1, sh2, sc2, gt2 = mods[0]
    proj = norm_inproj(x, prm['norm_mix_g'][0], sc1, sh1, wts['even_in'])
    ya, n_conv, n_ssd = ssd_mixer(proj, conv_st, ssd_st, prm['ssd_conv_w'][0], prm['ssd_conv_b'][0],
                                  prm['ssd_dt_bias'][0], prm['ssd_a_log'][0], prm['ssd_d'][0],
                                  prm['ssd_norm_g'][0])
    yb, n_hg = hgrn_mixer(proj, hg_st, wts['hgrn_lb'], prm['hgrn_norm_g'][0])
    x, hn = mix_outproj(ya, yb, wts['even_out'], x, gt1, prm['norm_ffn_g'][0], sc2, sh2)
    hn2 = hn.reshape(n, D)
    gates = moe_router(hn2, prm['moe_router_w'][0], prm['moe_router_bias'][0])
    ffn = moe_experts(hn2, gates, 0, prm['moe_w_gate'], prm['moe_w_up'], prm['moe_w_down'],
                      prm['moe_sh_gate'], prm['moe_sh_up'], prm['moe_sh_down'])
    x = residual(x, ffn, gt2, prm['final_norm_g'], False)

    sh1, sc1, gt1, sh2, sc2, gt2 = mods[1]
    proj = norm_inproj(x, prm['norm_mix_g'][1], sc1, sh1, wts['odd_in'])
    yc, n_shift, n_rw = rwkv_mixer(proj, shift_st, rw_st, prm['rwkv_mu'][0], prm['rwkv_w0'][0], prm['rwkv_w2'][0],
                                   prm['rwkv_a0'][0], prm['rwkv_a2'][0], prm['rwkv_g2'][0], prm['rwkv_k_k'][0],
                                   prm['rwkv_k_a'][0], prm['rwkv_r_k'][0], prm['rwkv_ln_g'][0],
                                   prm['rwkv_ln_b'][0])
    logf = fox_logf(proj, prm['fox_f_bias'][0])
    kf = proj[:, :, ODD_K_BLOCK * D:(ODD_K_BLOCK + 1) * D]
    vf = proj[:, :, ODD_V_BLOCK * D:(ODD_V_BLOCK + 1) * D]
    lane_pad = lambda a: jnp.pad(a, ((0, 0), (0, 0), (0, LANES - FOX_HEADS)))
    if fk_past is None:
        past = 0
        cum = time_cumsum(lane_pad(logf))
        ka = fox_augment(proj, ODD_K_BLOCK, cum, 'k')
        va = fox_augment(proj, ODD_V_BLOCK, cum, 'v')
    else:
        past = fk_past.shape[1]
        pad = (-(past + T)) % LANES
        lf_all = jnp.concatenate([fl_past, logf, jnp.zeros((B, pad, FOX_HEADS), F32)], axis=1)
        cum = time_cumsum(lane_pad(lf_all))
        k_arr = jnp.concatenate([fk_past.reshape(B, past, D), kf, jnp.zeros((B, pad, D), F32)], axis=1)
        v_arr = jnp.concatenate([fv_past.reshape(B, past, D), vf, jnp.zeros((B, pad, D), F32)], axis=1)
        ka = fox_augment(k_arr, 0, cum, 'k')
        va = fox_augment(v_arr, 0, cum, 'v')
    qa = fox_augment(proj, ODD_Q_BLOCK, cum[:, past:past + T], 'q')
    yd = fox_attention(qa, ka, va, past, T)
    x, hn = mix_outproj(yc, yd, wts['odd_out'], x, gt1, prm['norm_ffn_g'][1], sc2, sh2)
    hn2 = hn.reshape(n, D)
    gates = moe_router(hn2, prm['moe_router_w'][1], prm['moe_router_bias'][1])
    ffn = moe_experts(hn2, gates, 1, prm['moe_w_gate'], prm['moe_w_up'], prm['moe_w_down'],
                      prm['moe_sh_gate'], prm['moe_sh_up'], prm['moe_sh_down'])
    y = residual(x, ffn, gt2, prm['final_norm_g'], True)
    hs = (B, T, FOX_HEADS, FOX_HD)
    return (y, n_conv[None], n_ssd[None], n_hg[None], n_shift[None], n_rw[None],
            kf.reshape(hs)[None], vf.reshape(hs)[None], logf[None])


def _prep_weights(prm):
    ein = prm['even_in_w'][0]
    z, xbc, dt, hq, hf, hi, hgate = jnp.split(ein, [1024, 2304, 2320, 3344, 4368, 5392], axis=1)
    even_in = jnp.concatenate([z, hq, hf, hi, hgate, xbc, dt, jnp.zeros((D, LANES - SSD_HEADS), F32)], axis=1)
    rkv, lora, qkv, fl = jnp.split(prm['odd_in_w'][0], [3 * D, RW_COLS, RW_COLS + 3 * D], axis=1)
    odd_in = jnp.concatenate([rkv, qkv, lora, fl, jnp.zeros((D, LANES - FOX_HEADS), F32)], axis=1)
    lb = jnp.cumsum(jax.nn.softmax(prm['hgrn_lb_logits'].astype(F32), axis=0), axis=0)[0]
    return {
        'even_in': even_in.astype(BF16),
        'odd_in': odd_in.astype(BF16),
        'even_out': prm['even_out_w'][0].astype(BF16),
        'odd_out': prm['odd_out_w'][0].astype(BF16),
        'hgrn_lb': lb,
    }


def kernel(x_prompt, x_sample, c_prompt, c_sample, state_ssd_conv, state_ssd, state_hgrn, state_rwkv_shift, state_rwkv, cache_fox_k, cache_fox_v, cache_fox_logf, ada_w, ada_b, norm_mix_g, norm_ffn_g, final_norm_g, even_in_w, even_out_w, ssd_conv_w, ssd_conv_b, ssd_dt_bias, ssd_a_log, ssd_d, ssd_norm_g, hgrn_lb_logits, hgrn_norm_g, odd_in_w, odd_out_w, rwkv_mu, rwkv_w0, rwkv_w2, rwkv_a0, rwkv_a2, rwkv_g2, rwkv_k_k, rwkv_k_a, rwkv_r_k, rwkv_ln_g, rwkv_ln_b, fox_f_bias, moe_router_w, moe_router_bias, moe_w_gate, moe_w_up, moe_w_down, moe_sh_gate, moe_sh_up, moe_sh_down):
    prm = dict(ada_w=ada_w, ada_b=ada_b, norm_mix_g=norm_mix_g, norm_ffn_g=norm_ffn_g, final_norm_g=final_norm_g,
               even_in_w=even_in_w, even_out_w=even_out_w, ssd_conv_w=ssd_conv_w, ssd_conv_b=ssd_conv_b,
               ssd_dt_bias=ssd_dt_bias, ssd_a_log=ssd_a_log, ssd_d=ssd_d, ssd_norm_g=ssd_norm_g,
               hgrn_lb_logits=hgrn_lb_logits, hgrn_norm_g=hgrn_norm_g,
               odd_in_w=odd_in_w, odd_out_w=odd_out_w, rwkv_mu=rwkv_mu, rwkv_w0=rwkv_w0, rwkv_w2=rwkv_w2,
               rwkv_a0=rwkv_a0, rwkv_a2=rwkv_a2, rwkv_g2=rwkv_g2, rwkv_k_k=rwkv_k_k, rwkv_k_a=rwkv_k_a,
               rwkv_r_k=rwkv_r_k, rwkv_ln_g=rwkv_ln_g, rwkv_ln_b=rwkv_ln_b, fox_f_bias=fox_f_bias,
               moe_router_w=moe_router_w, moe_router_bias=moe_router_bias, moe_w_gate=moe_w_gate,
               moe_w_up=moe_w_up, moe_w_down=moe_w_down, moe_sh_gate=moe_sh_gate, moe_sh_up=moe_sh_up,
               moe_sh_down=moe_sh_down)
    wts = _prep_weights(prm)
    bp, bs = x_prompt.shape[0], x_sample.shape[0]
    depth = ada_w.shape[0]
    c_all = jnp.concatenate([c_prompt, c_sample], axis=0)
    rows = c_all.shape[0]
    c_all = jnp.pad(c_all, ((0, (-rows) % SUBLANES), (0, 0)))
    mod = ada_mod(c_all, ada_w, ada_b)

    def mods_for(lo, hi):
        return [tuple(m[:, None, :] for m in jnp.split(mod[l, lo:hi], 6, axis=-1)) for l in range(depth)]

    z = lambda *s: jnp.zeros(s, F32)
    out_p = _trunk(x_prompt, mods_for(0, bp), prm, wts,
                   z(bp, SSD_CONV - 1, SSD_CONV_DIM), z(bp, SSD_HEADS, SSD_HD, SSD_STATE),
                   z(bp, HG_HEADS, HG_DK, HG_DV), z(bp, RW_COLS), z(bp, RW_HEADS, RW_N, RW_N),
                   None, None, None)
    out_s = _trunk(x_sample, mods_for(bp, bp + bs), prm, wts,
                   state_ssd_conv[0], state_ssd[0], state_hgrn[0], state_rwkv_shift[0], state_rwkv[0],
                   cache_fox_k[0], cache_fox_v[0], cache_fox_logf[0])
    return (out_p[0], out_s[0]) + tuple(out_p[1:]) + tuple(out_s[1:])
```

```python
import functools

import jax
import jax.numpy as jnp
from jax import lax
from jax.experimental import pallas as pl
from jax.experimental.pallas import tpu as pltpu

F32 = jnp.float32
BF16 = jnp.bfloat16
HI = lax.Precision.HIGHEST

D = 1024
EPS = 1e-6
LANES = 128
SUBLANES = 8

SSD_HEADS = 16
SSD_HD = 64
SSD_STATE = 64
SSD_CONV = 4
SSD_CONV_DIM = 1280
SSD_L = 64
HG_HEADS = 8
HG_DK = 128
HG_DV = 128
HG_L = 64
RW_HEADS = 16
RW_N = 64
RW_COLS = 3328
RW_LN_EPS = 64e-5
RW_L = 64
FOX_HEADS = 16
FOX_HD = 64
FOX_AW = FOX_HEADS * LANES
FOX_TQ = 1024
FOX_TK = 1024
LOG2E = 1.4426950408889634
ODD_Q_BLOCK, ODD_K_BLOCK, ODD_V_BLOCK = 3, 4, 5
ODD_LORA_BLOCK = 24
ODD_FL_BLOCK = 50
RW_LORA = 256
N_EXPERTS = 64
TOP_K = 6
N_GROUPS = 8
TOPK_GROUPS = 4
D_EXPERT = 256
ROUTED_SCALE = 2.5
PROJ_COLS = 6528
PROJ_TN = 2176

NEG_INF = float("-inf")


def _cparams(sem, vmem_mb=None):
    kw = dict(dimension_semantics=sem)
    if vmem_mb is not None:
        kw["vmem_limit_bytes"] = vmem_mb * 1024 * 1024
    return pltpu.CompilerParams(**kw)


def _mm(a, b):
    return jnp.dot(a.astype(BF16), b.astype(BF16), preferred_element_type=F32)


def _mm_nt(a, b):
    return lax.dot_general(a.astype(BF16), b.astype(BF16), (((1,), (1,)), ((), ())),
                           preferred_element_type=F32)


def _mm_hi(a, b):
    return jnp.dot(a, b, precision=HI, preferred_element_type=F32)


def _mm_01(sel_bf, x):
    x1 = x.astype(BF16)
    r1 = x - x1.astype(F32)
    x2 = r1.astype(BF16)
    x3 = (r1 - x2.astype(F32)).astype(BF16)
    return (jnp.dot(sel_bf, x1, preferred_element_type=F32) + jnp.dot(sel_bf, x2, preferred_element_type=F32)
            + jnp.dot(sel_bf, x3, preferred_element_type=F32))


def _sigmoid(x):
    return 1.0 / (1.0 + jnp.exp(-x))


def _silu(x):
    return x * _sigmoid(x)


def _softplus(x):
    return jnp.maximum(x, 0.0) + jnp.log1p(jnp.exp(-jnp.abs(x)))


def _row_tiles(G, gs, target):
    if gs >= target:
        assert gs % target == 0
        return 1, target
    gb = max(1, min(G, target // gs))
    while G % gb:
        gb -= 1
    return gb, gs


def _ada_kernel(c_ref, w_ref, b_ref, o_ref):
    c = c_ref[...]
    o_ref[0] = _mm(_silu(c), w_ref[0]) + b_ref[0]


def ada_mod(c_all, ada_w, ada_b):
    depth = ada_w.shape[0]
    bp = c_all.shape[0]
    tn = 1024
    return pl.pallas_call(
        _ada_kernel,
        grid=(depth, 6 * D // tn),
        in_specs=[
            pl.BlockSpec((bp, D), lambda l, j: (0, 0)),
            pl.BlockSpec((1, D, tn), lambda l, j: (l, 0, j)),
            pl.BlockSpec((1, 1, tn), lambda l, j: (l, 0, j)),
        ],
        out_specs=pl.BlockSpec((1, bp, tn), lambda l, j: (l, 0, j)),
        out_shape=jax.ShapeDtypeStruct((depth, bp, 6 * D), F32),
        compiler_params=_cparams(("arbitrary", "arbitrary")),
        name="ada_mod",
    )(c_all, ada_w, ada_b.reshape(depth, 1, 6 * D))


def _rms_mod(x, g, sc, sh):
    y = x * lax.rsqrt(jnp.mean(x * x, axis=-1, keepdims=True) + EPS)
    return (y * g) * (1.0 + sc) + sh


def _inproj_kernel(x_ref, g_ref, sc_ref, sh_ref, w_ref, o_ref, xn_ref):
    @pl.when(pl.program_id(1) == 0)
    def _():
        hn = _rms_mod(x_ref[...], g_ref[...], sc_ref[...], sh_ref[...])
        xn_ref[...] = hn.reshape(xn_ref.shape).astype(BF16)

    o_ref[...] = jnp.dot(xn_ref[...], w_ref[...], preferred_element_type=F32)


def norm_inproj(x, g, sc, sh, w_bf):
    G, gs, _ = x.shape
    gb, tb = _row_tiles(G, gs, 512)
    nt = gs // tb
    tt = gb * tb
    n = G * gs
    out = pl.pallas_call(
        _inproj_kernel,
        grid=(n // tt, PROJ_COLS // PROJ_TN),
        in_specs=[
            pl.BlockSpec((gb, tb, D), lambda i, j: (i // nt, i % nt, 0)),
            pl.BlockSpec((1, D), lambda i, j: (0, 0)),
            pl.BlockSpec((gb, 1, D), lambda i, j: (i // nt, 0, 0)),
            pl.BlockSpec((gb, 1, D), lambda i, j: (i // nt, 0, 0)),
            pl.BlockSpec((D, PROJ_TN), lambda i, j: (0, j)),
        ],
        out_specs=pl.BlockSpec((tt, PROJ_TN), lambda i, j: (i, j)),
        out_shape=jax.ShapeDtypeStruct((n, PROJ_COLS), F32),
        scratch_shapes=[pltpu.VMEM((tt, D), BF16)],
        compiler_params=_cparams(("arbitrary", "arbitrary"), 48),
        name="norm_inproj",
    )(x, g.reshape(1, D), sc, sh, w_bf)
    return out.reshape(G, gs, PROJ_COLS)


def _outproj_kernel(ya_ref, yb_ref, w_ref, x_ref, gt_ref, g_ref, sc_ref, sh_ref, xo_ref, hn_ref):
    gb, tb, _ = x_ref.shape
    out = (jnp.dot(ya_ref[...].astype(BF16), w_ref[0:D, :], preferred_element_type=F32)
           + jnp.dot(yb_ref[...].astype(BF16), w_ref[D:2 * D, :], preferred_element_type=F32))
    x = x_ref[...] + gt_ref[...] * out.reshape(gb, tb, D)
    xo_ref[...] = x
    hn_ref[...] = _rms_mod(x, g_ref[...], sc_ref[...], sh_ref[...])


def mix_outproj(ya, yb, w_bf, x, gt, g, sc, sh):
    G, gs, _ = x.shape
    gb, tb = _row_tiles(G, gs, 512)
    nt = gs // tb
    tt = gb * tb
    n = G * gs
    row3 = pl.BlockSpec((gb, tb, D), lambda i: (i // nt, i % nt, 0))
    mod3 = pl.BlockSpec((gb, 1, D), lambda i: (i // nt, 0, 0))
    return pl.pallas_call(
        _outproj_kernel,
        grid=(n // tt,),
        in_specs=[
            pl.BlockSpec((tt, D), lambda i: (i, 0)),
            pl.BlockSpec((tt, D), lambda i: (i, 0)),
            pl.BlockSpec((2 * D, D), lambda i: (0, 0)),
            row3, mod3,
            pl.BlockSpec((1, D), lambda i: (0, 0)),
            mod3, mod3,
        ],
        out_specs=[row3, row3],
        out_shape=[jax.ShapeDtypeStruct((G, gs, D), F32), jax.ShapeDtypeStruct((G, gs, D), F32)],
        compiler_params=_cparams(("arbitrary",), 48),
        name="mix_outproj",
    )(ya.reshape(n, D), yb.reshape(n, D), w_bf, x, gt, g.reshape(1, D), sc, sh)


def _ssd_kernel(z_ref, xbc_ref, dt_ref, conv0_ref, st0_ref, cw_ref, cb_ref, dtb_ref, alog_ref, dfull_ref,
                ng_ref, y_ref, convo_ref, sto_ref, ext_ref, st_ref, yacc_ref):
    L = SSD_L
    c = pl.program_id(1)

    bd_r = lax.broadcasted_iota(jnp.int32, (LANES, LANES), 0) // SSD_HD
    bd_c = lax.broadcasted_iota(jnp.int32, (LANES, LANES), 1) // SSD_HD
    blockdiag = bd_r == bd_c

    @pl.when(c == 0)
    def _():
        ext_ref[0:SUBLANES, :] = conv0_ref[0]
        for p in range(SSD_HEADS // 2):
            x = st0_ref[0, p]
            st_ref[p] = jnp.where(blockdiag, jnp.concatenate([x, x], axis=1), 0.0)

    u = xbc_ref[0]
    ext_ref[SUBLANES:SUBLANES + L, :] = u
    acc = cb_ref[...] + u * cw_ref[SSD_CONV - 1:SSD_CONV, :]
    for i in range(SSD_CONV - 1):
        sh = SSD_CONV - 1 - i
        acc = acc + ext_ref[SUBLANES - sh:SUBLANES - sh + L, :] * cw_ref[i:i + 1, :]
    ext_ref[0:SUBLANES, :] = ext_ref[L:L + SUBLANES, :]
    xbc = _silu(acc)
    xs = xbc[:, 0:D]
    bm2 = xbc[:, D:D + LANES]
    cm2 = xbc[:, D + LANES:D + 2 * LANES]

    dt = _softplus(dt_ref[0] + dtb_ref[...])
    a = dt * (-jnp.exp(alog_ref[...]))
    row = lax.broadcasted_iota(jnp.int32, (L, L), 0)
    col = lax.broadcasted_iota(jnp.int32, (L, L), 1)
    tril = (col <= row).astype(F32)
    triu = (row <= col).astype(F32)
    cum = _mm_hi(tril, a)
    cum_t = _mm_hi(a.T, triu)
    eh = lax.broadcasted_iota(jnp.int32, (LANES, D), 0)
    ec = lax.broadcasted_iota(jnp.int32, (LANES, D), 1)
    expand = (ec // SSD_HD == eh).astype(F32)
    dt_full = _mm_hi(dt, expand)
    cum_full = _mm_hi(cum, expand)
    ecum_full = jnp.exp(cum_full)
    last_full = cum_full[L - 1:L, :]
    w_full = jnp.exp(last_full - cum_full)
    xdt = xs * dt_full

    lane = lax.broadcasted_iota(jnp.int32, (L, LANES), 1)
    lo = lane < SSD_HD
    tri2 = (lane % L) <= lax.broadcasted_iota(jnp.int32, (L, LANES), 0)
    bm_r = pltpu.roll(bm2, SSD_HD, 1)
    cm_r = pltpu.roll(cm2, SSD_HD, 1)
    bmstack = jnp.concatenate([bm2, bm2], axis=0)

    for g in range(2):
        grp_lo = lo if g == 0 else jnp.logical_not(lo)
        cm_m = jnp.where(grp_lo, cm2, 0.0)
        cb_dup = _mm_nt(cm_m, bmstack)
        cm_dup = jnp.where(grp_lo, cm2, cm_r)
        bm_dup = jnp.where(grp_lo, bm2, bm_r)
        for pp in range(4):
            p = 4 * g + pp
            sl = slice(LANES * p, LANES * (p + 1))
            cum_pair = cum_full[:, sl]
            cum_row = jnp.concatenate([cum_t[2 * p:2 * p + 1, :], cum_t[2 * p + 1:2 * p + 2, :]], axis=1)
            s2 = cb_dup * jnp.exp(jnp.where(tri2, cum_pair - cum_row, NEG_INF))
            x2 = xdt[:, sl]
            xstack = jnp.concatenate([jnp.where(lo, x2, 0.0), jnp.where(lo, 0.0, x2)], axis=0)
            hp = st_ref[p]
            lhs = jnp.concatenate([s2, cm_dup * ecum_full[:, sl]], axis=1)
            rhs = jnp.concatenate([xstack, hp], axis=0)
            yacc_ref[:, sl] = _mm(lhs, rhs)
            bw_t = (bm_dup * w_full[:, sl]).T
            upd = _mm(bw_t, x2)
            st_ref[p] = jnp.where(blockdiag, hp * ecum_full[L - 1:L, sl] + upd, 0.0)

    y = yacc_ref[...] + xs * dfull_ref[...]
    z = z_ref[0]
    yz = y * _silu(z)
    y_ref[0] = yz * lax.rsqrt(jnp.mean(yz * yz, axis=-1, keepdims=True) + EPS) * ng_ref[...]

    @pl.when(c == pl.num_programs(1) - 1)
    def _():
        convo_ref[0] = ext_ref[0:SUBLANES, :]
        first = lax.broadcasted_iota(jnp.int32, (LANES, SSD_HD), 0) < SSD_STATE
        for p in range(SSD_HEADS // 2):
            hp = st_ref[p]
            sto_ref[0, p] = jnp.where(first, hp[:, 0:SSD_HD], hp[:, SSD_HD:2 * SSD_HD])


def ssd_mixer(proj, conv_state, ssd_state, conv_w, conv_b, dt_bias, a_log, d_skip, norm_g):
    B, T, _ = proj.shape
    L = SSD_L
    assert T % L == 0
    H, P, N = SSD_HEADS, SSD_HD, SSD_STATE
    conv0 = jnp.pad(conv_state, ((0, 0), (SUBLANES - (SSD_CONV - 1), 0), (0, 0)))
    st0 = jnp.swapaxes(ssd_state, -1, -2).reshape(B, H // 2, 2 * N, P)
    st_spec = pl.BlockSpec((1, H // 2, 2 * N, P), lambda b, c: (b, 0, 0, 0))
    cw =jnp.pad(conv_w, ((0, SUBLANES - SSD_CONV), (0, 0)))
    pad16 = lambda v: jnp.pad(v.reshape(1, H), ((0, 0), (0, LANES - H)))
    dfull = jnp.repeat(d_skip, P).reshape(1, D)
    vec = lambda w: pl.BlockSpec((1, w), lambda b, c: (0, 0))
    y, convo, sto = pl.pallas_call(
        _ssd_kernel,
        grid=(B, T // L),
        in_specs=[
            pl.BlockSpec((1, L, D), lambda b, c: (b, c, 0)),
            pl.BlockSpec((1, L, SSD_CONV_DIM), lambda b, c: (b, c, 4)),
            pl.BlockSpec((1, L, LANES), lambda b, c: (b, c, 50)),
            pl.BlockSpec((1, SUBLANES, SSD_CONV_DIM), lambda b, c: (b, 0, 0)),
            st_spec,
            pl.BlockSpec((SUBLANES, SSD_CONV_DIM), lambda b, c: (0, 0)),
            vec(SSD_CONV_DIM), vec(LANES), vec(LANES), vec(D), vec(D),
        ],
        out_specs=[
            pl.BlockSpec((1, L, D), lambda b, c: (b, c, 0)),
            pl.BlockSpec((1, SUBLANES, SSD_CONV_DIM), lambda b, c: (b, 0, 0)),
            st_spec,
        ],
        out_shape=[
            jax.ShapeDtypeStruct((B, T, D), F32),
            jax.ShapeDtypeStruct((B, SUBLANES, SSD_CONV_DIM), F32),
            jax.ShapeDtypeStruct((B, H // 2, 2 * N, P), F32),
        ],
        scratch_shapes=[
            pltpu.VMEM((SUBLANES + L, SSD_CONV_DIM), F32),
            pltpu.VMEM((H // 2, LANES, LANES), F32),
            pltpu.VMEM((L, D), F32),
        ],
        compiler_params=_cparams(("arbitrary", "arbitrary"), 48),
        name="ssd_mixer",
    )(proj, proj, proj, conv0, st0, cw, conv_b.reshape(1, SSD_CONV_DIM), pad16(dt_bias), pad16(a_log),
      dfull, norm_g.reshape(1, D))
    return y, convo[:, SUBLANES - (SSD_CONV - 1):], jnp.swapaxes(sto.reshape(B, H, N, P), -1, -2)


def _hgrn_kernel(q_ref, f_ref, i_ref, gate_ref, st0_ref, lb_ref, ng_ref, y_ref, sto_ref, st_ref):
    L = HG_L
    c = pl.program_id(1)

    @pl.when(c == 0)
    def _():
        st_ref[...] = st0_ref[0]

    lb = lb_ref[...]
    f = lb + (1.0 - lb) * _sigmoid(f_ref[0])
    logf = jnp.log(f)
    q = _silu(q_ref[0]) * (HG_DK ** -0.5)
    k = 1.0 - f
    v = i_ref[0]
    row = lax.broadcasted_iota(jnp.int32, (L, L), 0)
    col = lax.broadcasted_iota(jnp.int32, (L, L), 1)
    cum = _mm_01(jnp.where(col <= row, 1.0, 0.0).astype(BF16), logf)
    qe = q * jnp.exp(cum)
    last = cum[L - 1:L, :]
    kw = k * jnp.exp(last - cum)
    elast = jnp.exp(last)

    halves = [L >> (i + 1) for i in range(L.bit_length() - 1)]
    small = [hs for hs in halves if 2 * hs <= SUBLANES]
    rsel = lax.broadcasted_iota(jnp.int32, (L * len(small), L), 0)
    csel = lax.broadcasted_iota(jnp.int32, (L * len(small), L), 1)
    ref_of = jnp.zeros_like(rsel)
    for i, hs in enumerate(small):
        ref_of = jnp.where(rsel // L == i, ((rsel - L * i) // (2 * hs)) * (2 * hs) + hs - 1, ref_of)
    ref_small = _mm_01(jnp.where(csel == ref_of, 1.0, 0.0).astype(BF16), cum)
    rvec = lax.broadcasted_iota(jnp.int32, (L, 1), 0)
    qs, ks, pair_mask = [], [], []
    for hs in halves:
        if hs in small:
            i = small.index(hs)
            cum_ref = ref_small[L * i:L * (i + 1)]
        else:
            cum_ref = jnp.concatenate(
                [jnp.broadcast_to(cum[b + hs - 1:b + hs, :], (2 * hs, D)) for b in range(0, L, 2 * hs)], axis=0)
        upper = (rvec % (2 * hs)) >= hs
        d = cum - cum_ref
        e = jnp.exp(jnp.where(upper, d, -d))
        qs.append((q * e).astype(BF16))
        ks.append((k * e).astype(BF16))
        pair_mask.append(jnp.where((row // (2 * hs) == col // (2 * hs)) & (row % (2 * hs) >= hs)
                                   & (col % (2 * hs) < hs), 1.0, 0.0))
    eye = jnp.where(row == col, 1.0, 0.0)
    qb, kb = q.astype(BF16), k.astype(BF16)

    heads = [slice(HG_DK * h, HG_DK * (h + 1)) for h in range(HG_HEADS)]
    scores = []
    for sl in heads:
        s = eye * _mm_nt(qb[:, sl], kb[:, sl])
        for i in range(len(halves)):
            s = s + pair_mask[i] * _mm_nt(qs[i][:, sl], ks[i][:, sl])
        scores.append(s.astype(BF16))
    vb = v.astype(BF16)
    outs = []
    for h, sl in enumerate(heads):
        st = st_ref[h]
        outs.append(_mm_nt(qe[:, sl], st) + jnp.dot(scores[h], vb[:, sl], preferred_element_type=F32))
        st_ref[h] = st * elast[:, sl] + _mm(v[:, sl].T, kw[:, sl])
    for (h, sl), o in zip(enumerate(heads), outs):
        on = o * lax.rsqrt(jnp.mean(o * o, axis=-1, keepdims=True) + EPS) * ng_ref[:, sl]
        y_ref[0, :, sl] = on * _silu(gate_ref[0, :, sl])

    @pl.when(c == pl.num_programs(1) - 1)
    def _():
        sto_ref[0] = st_ref[...]


def hgrn_mixer(proj, hg_state, lb, norm_g):
    B, T, _ = proj.shape
    L = HG_L
    assert T % L == 0
    st0 = jnp.swapaxes(hg_state, -1, -2)
    col = lambda j: pl.BlockSpec((1, L, D), lambda b, c: (b, c, j))
    st_spec = pl.BlockSpec((1, HG_HEADS, HG_DV, HG_DK), lambda b, c: (b, 0, 0, 0))
    vec = pl.BlockSpec((1, D), lambda b, c: (0, 0))
    y, sto = pl.pallas_call(
        _hgrn_kernel,
        grid=(B, T // L),
        in_specs=[col(1), col(2), col(3), col(4), st_spec, vec, vec],
        out_specs=[pl.BlockSpec((1, L, D), lambda b, c: (b, c, 0)), st_spec],
        out_shape=[jax.ShapeDtypeStruct((B, T, D), F32),
                   jax.ShapeDtypeStruct((B, HG_HEADS, HG_DV, HG_DK), F32)],
        scratch_shapes=[pltpu.VMEM((HG_HEADS, HG_DV, HG_DK), F32)],
        compiler_params=_cparams(("arbitrary", "arbitrary")),
        name="hgrn_mixer",
    )(proj, proj, proj, proj, st0, lb.reshape(1, D), norm_g.reshape(1, D))
    return y, jnp.swapaxes(sto, -1, -2)


def _first_max_onehot(vals, iota, axis, size):
    m = jnp.max(vals, axis=axis, keepdims=True)
    idx = jnp.min(jnp.where(vals == m, iota, size), axis=axis, keepdims=True)
    return iota == idx, m


def _router_kernel(x_ref, wt_ref, bias_ref, o_ref):
    tt = x_ref.shape[0]
    per = N_EXPERTS // N_GROUPS
    logits = lax.dot_general(wt_ref[...], x_ref[...], (((1,), (1,)), ((), ())),
                             precision=HI, preferred_element_type=F32)
    scores = _sigmoid(logits)
    biased = scores + bias_ref[...]
    b3 = biased.reshape(N_GROUPS, per, tt)
    i3 = lax.broadcasted_iota(jnp.int32, (N_GROUPS, per, tt), 1)
    oh, m1 = _first_max_onehot(b3, i3, 1, per)
    m2 = jnp.max(jnp.where(oh, NEG_INF, b3), axis=1, keepdims=True)
    grp = (m1 + m2).reshape(N_GROUPS, tt)
    ig = lax.broadcasted_iota(jnp.int32, (N_GROUPS, tt), 0)
    gsel = jnp.zeros((N_GROUPS, tt), jnp.bool_)
    for _ in range(TOPK_GROUPS):
        oh, _m = _first_max_onehot(grp, ig, 0, N_GROUPS)
        gsel = jnp.logical_or(gsel, oh)
        grp = jnp.where(oh, NEG_INF, grp)
    gmask = jnp.where(gsel, 1.0, 0.0).reshape(N_GROUPS, 1, tt)
    masked = jnp.where(gmask > 0.0, b3, NEG_INF).reshape(N_EXPERTS, tt)
    ie = lax.broadcasted_iota(jnp.int32, (N_EXPERTS, tt), 0)
    esel = jnp.zeros((N_EXPERTS, tt), F32)
    for _ in range(TOP_K):
        oh, _m = _first_max_onehot(masked, ie, 0, N_EXPERTS)
        esel = jnp.where(oh, 1.0, esel)
        masked = jnp.where(oh, NEG_INF, masked)
    wsel = scores * esel
    gates = wsel / jnp.sum(wsel, axis=0, keepdims=True) * ROUTED_SCALE
    o_ref[...] = gates.T


def moe_router(hn2d, router_w, router_bias):
    n = hn2d.shape[0]
    tt = min(n, 512)
    return pl.pallas_call(
        _router_kernel,
        grid=(n // tt,),
        in_specs=[
            pl.BlockSpec((tt, D), lambda i: (i, 0)),
            pl.BlockSpec((N_EXPERTS, D), lambda i: (0, 0)),
            pl.BlockSpec((N_EXPERTS, 1), lambda i: (0, 0)),
        ],
        out_specs=pl.BlockSpec((tt, N_EXPERTS), lambda i: (i, 0)),
        out_shape=jax.ShapeDtypeStruct((n, N_EXPERTS), F32),
        compiler_params=_cparams(("arbitrary",)),
        name="moe_router",
    )(hn2d, router_w.T, router_bias.reshape(N_EXPERTS, 1))


def _ffn(xb, wg, wu, wd, gate_col):
    h = _silu(jnp.dot(xb, wg.astype(BF16), preferred_element_type=F32)) * jnp.dot(
        xb, wu.astype(BF16), preferred_element_type=F32)
    if gate_col is not None:
        h = h * gate_col
    return jnp.dot(h.astype(BF16), wd.astype(BF16), preferred_element_type=F32)


def _moe_kernel(x_ref, gates_ref, wg_ref, wu_ref, wd_ref, sg_ref, su_ref, sd_ref, o_ref, xb_ref, acc_ref):
    e = pl.program_id(1)

    @pl.when(e == 0)
    def _():
        xb_ref[...] = x_ref[...].astype(BF16)
        acc_ref[...] = _ffn(xb_ref[...], sg_ref[0], su_ref[0], sd_ref[0], None)

    gates = gates_ref[...]
    lane = lax.broadcasted_iota(jnp.int32, gates.shape, 1)
    gate_col = jnp.sum(jnp.where(lane == e, gates, 0.0), axis=1, keepdims=True)
    acc_ref[...] += _ffn(xb_ref[...], wg_ref[0, 0], wu_ref[0, 0], wd_ref[0, 0], gate_col)

    @pl.when(e == pl.num_programs(1) - 1)
    def _():
        o_ref[...] = acc_ref[...]


def moe_experts(hn2d, gates, l, w_gate, w_up, w_down, sh_gate, sh_up, sh_down):
    n = hn2d.shape[0]
    tt = min(n, 1024)
    return pl.pallas_call(
        _moe_kernel,
        grid=(n // tt, N_EXPERTS),
        in_specs=[
            pl.BlockSpec((tt, D), lambda i, e: (i, 0)),
            pl.BlockSpec((tt, N_EXPERTS), lambda i, e: (i, 0)),
            pl.BlockSpec((1, 1, D, D_EXPERT), lambda i, e: (l, e, 0, 0)),
            pl.BlockSpec((1, 1, D, D_EXPERT), lambda i, e: (l, e, 0, 0)),
            pl.BlockSpec((1, 1, D_EXPERT, D), lambda i, e: (l, e, 0, 0)),
            pl.BlockSpec((1, D, D_EXPERT), lambda i, e: (l, 0, 0)),
            pl.BlockSpec((1, D, D_EXPERT), lambda i, e: (l, 0, 0)),
            pl.BlockSpec((1, D_EXPERT, D), lambda i, e: (l, 0, 0)),
        ],
        out_specs=pl.BlockSpec((tt, D), lambda i, e: (i, 0)),
        out_shape=jax.ShapeDtypeStruct((n, D), F32),
        scratch_shapes=[pltpu.VMEM((tt, D), BF16), pltpu.VMEM((tt, D), F32)],
        compiler_params=_cparams(("arbitrary", "arbitrary"), 56),
        name="moe_experts",
    )(hn2d, gates, w_gate, w_up, w_down, sh_gate, sh_up, sh_down)


def _residual_kernel(x_ref, f_ref, gt_ref, g_ref, o_ref, *, final_norm):
    x = x_ref[...] + gt_ref[...] * f_ref[...]
    if final_norm:
        x = x * lax.rsqrt(jnp.mean(x * x, axis=-1, keepdims=True) + EPS) * g_ref[...]
    o_ref[...] = x


def residual(x, ffn, gt, g, final_norm):
    G, gs, _ = x.shape
    gb, tb = _row_tiles(G, gs, 512)
    nt = gs // tb
    row3 = pl.BlockSpec((gb, tb, D), lambda i: (i // nt, i % nt, 0))
    return pl.pallas_call(
        functools.partial(_residual_kernel, final_norm=final_norm),
        grid=(G * gs // (gb * tb),),
        in_specs=[row3, row3, pl.BlockSpec((gb, 1, D), lambda i: (i // nt, 0, 0)),
                  pl.BlockSpec((1, D), lambda i: (0, 0))],
        out_specs=row3,
        out_shape=jax.ShapeDtypeStruct((G, gs, D), F32),
        compiler_params=_cparams(("arbitrary",)),
        name="residual",
    )(x, ffn.reshape(G, gs, D), gt, g.reshape(1, D))


def _head_sum(x):
    L = x.shape[0]
    lo = lax.broadcasted_iota(jnp.int32, (L, LANES), 1) < RW_N
    outs = []
    for p in range(RW_HEADS // 2):
        x2 = x[:, LANES * p:LANES * (p + 1)]
        s_lo = jnp.sum(jnp.where(lo, x2, 0.0), axis=1, keepdims=True)
        s_hi = jnp.sum(jnp.where(lo, 0.0, x2), axis=1, keepdims=True)
        outs.append(jnp.where(lo, s_lo, s_hi))
    return jnp.concatenate(outs, axis=1)


def _rwkv_kernel(rkv_ref, lora_ref, shift0_ref, st0_ref, mu_ref, w0_ref, w2_ref, a0_ref, a2_ref, g2_ref, kk_ref, ka_ref,
                 rk_ref, lng_ref, lnb_ref, y_ref, shifto_ref, sto_ref,
                 ext_ref, st_ref, r_s, w_s, k_s, b_s, nkk_s, vt_s, ot_s, o_s):
    L = RW_L
    N = RW_N
    NP = RW_HEADS // 2
    c = pl.program_id(1)

    @pl.when(c == 0)
    def _():
        ext_ref[0:SUBLANES, :] = shift0_ref[0]
        for p in range(NP):
            st_ref[p] = jnp.concatenate([st0_ref[0, 2 * p], st0_ref[0, 2 * p + 1]], axis=1)

    pr = jnp.concatenate([rkv_ref[0], lora_ref[0]], axis=1)
    ext_ref[SUBLANES:SUBLANES + L, :] = pr
    prev = ext_ref[SUBLANES - 1:SUBLANES - 1 + L, :]
    ext_ref[0:SUBLANES, :] = ext_ref[L:L + SUBLANES, :]
    ps = pr + (prev - pr) * mu_ref[...]
    r = ps[:, 0:D]
    k = ps[:, D:2 * D]
    v = ps[:, 2 * D:3 * D]
    wa = ps[:, 3 * D:3 * D + LANES]
    gd = ps[:, 3 * D + LANES:3 * D + 2 * LANES]
    w = -_softplus(-(w0_ref[...] + _mm(jnp.tanh(wa), w2_ref[...]))) - 0.5
    wdec = jnp.exp(-jnp.exp(w))
    a = _sigmoid(a0_ref[...] + _mm(wa, a2_ref[...]))
    g = _mm(_sigmoid(gd), g2_ref[...])
    kk = k * kk_ref[...]
    kk = kk * lax.rsqrt(jnp.maximum(_head_sum(kk * kk), 1e-24))
    k = k * (1.0 + (a - 1.0) * ka_ref[...])
    bonus = _head_sum(r * k * rk_ref[...]) * v
    b = kk * a
    for p in range(NP):
        ps_ = slice(LANES * p, LANES * (p + 1))
        r_s[p] = r[:, ps_]
        w_s[p] = wdec[:, ps_]
        k_s[p] = k[:, ps_]
        b_s[p] = b[:, ps_]
        nkk_s[p] = -kk[:, ps_]
        v2 = v[:, ps_]
        vt_s[p] = jnp.concatenate([v2[:, 0:N], v2[:, N:2 * N]], axis=0).T
    ot_s[...] = jnp.zeros(ot_s.shape, F32)

    bd_r = lax.broadcasted_iota(jnp.int32, (2 * LANES, 2 * LANES), 0) // N
    bd_c = lax.broadcasted_iota(jnp.int32, (2 * LANES, 2 * LANES), 1) // N
    seg_ones = jnp.where(bd_r == bd_c, 1.0, 0.0).astype(BF16)

    def seg_sum(parts):
        half = len(parts) // 2
        outs = []
        for grp in (parts[:half], parts[half:]):
            x = jnp.concatenate([jnp.concatenate(grp[0::2], axis=0), jnp.concatenate(grp[1::2], axis=0)], axis=1)
            y = jnp.dot(x.astype(BF16), seg_ones, preferred_element_type=F32)
            for i in range(len(grp)):
                outs.append(y[N * (i // 2):N * (i // 2 + 1), LANES * (i % 2):LANES * (i % 2 + 1)])
        return outs

    lane_time = lax.broadcasted_iota(jnp.int32, (N, LANES), 1) % L
    row = lambda ref, p, t: ref[p, pl.ds(t, 1), :]

    def step(t, carry):
        tp = jnp.maximum(t - 1, 0)
        parts = []
        for p in range(NP):
            s = st_ref[p]
            parts += [s * row(nkk_s, p, t), jnp.where(lane_time == t, vt_s[p], 0.0), s * row(r_s, p, tp)]
        res = seg_sum(parts)
        for p in range(NP):
            sa, vcol, o_prev = res[3 * p:3 * p + 3]
            st_ref[p] = st_ref[p] * row(w_s, p, t) + sa * row(b_s, p, t) + vcol * row(k_s, p, t)
            ot_s[p] = jnp.where(lane_time == t - 1, o_prev, ot_s[p])
        return carry

    lax.fori_loop(0, L, step, 0, unroll=2)
    o_last = seg_sum([st_ref[p] * row(r_s, p, L - 1) for p in range(NP)])
    for p in range(NP):
        ot_s[p] = jnp.where(lane_time == L - 1, o_last[p], ot_s[p])

    for p in range(NP):
        ot = ot_s[p]
        mean = jnp.mean(ot, axis=0, keepdims=True)
        cen = ot - mean
        var = jnp.mean(cen * cen, axis=0, keepdims=True)
        on_t = (cen * lax.rsqrt(var + RW_LN_EPS)).T
        o_s[:, LANES * p:LANES * p + N] = on_t[0:L]
        o_s[:, LANES * p + N:LANES * (p + 1)] = on_t[L:2 * L]
    o = o_s[...] * lng_ref[...] + lnb_ref[...]
    y_ref[0] = (o + bonus) * g

    @pl.when(c == pl.num_programs(1) - 1)
    def _():
        shifto_ref[0] = ext_ref[0:SUBLANES, :]
        for p in range(NP):
            s = st_ref[p]
            sto_ref[0, 2 * p] = s[:, 0:N]
            sto_ref[0, 2 * p + 1] = s[:, N:2 * N]


def rwkv_mixer(proj, shift_state, rw_state, mu, w0, w2, a0, a2, g2, k_k, k_a, r_k, ln_g, ln_b):
    B, T, _ = proj.shape
    L = RW_L
    assert T % L == 0
    H, N = RW_HEADS, RW_N
    shift0 = jnp.pad(shift_state[:, None, :], ((0, 0), (SUBLANES - 1, 0), (0, 0)))
    w2p = jnp.pad(w2, ((0, LANES - w2.shape[0]), (0, 0)))
    a2p = jnp.pad(a2, ((LANES - a2.shape[0], 0), (0, 0)))
    vec = lambda w: pl.BlockSpec((1, w), lambda b, c: (0, 0))
    mat = pl.BlockSpec((LANES, D), lambda b, c: (0, 0))
    st_spec = pl.BlockSpec((1, H, N, N), lambda b, c: (b, 0, 0, 0))
    sh_spec = pl.BlockSpec((1, SUBLANES, RW_COLS), lambda b, c: (b, 0, 0))
    rows = pltpu.VMEM((H // 2, L, 2 * N), F32)
    cols = pltpu.VMEM((H // 2, N, 2 * L), F32)
    y, shifto, sto = pl.pallas_call(
        _rwkv_kernel,
        grid=(B, T // L),
        in_specs=[pl.BlockSpec((1, L, 3 * D), lambda b, c: (b, c, 0)),
                  pl.BlockSpec((1, L, RW_LORA), lambda b, c: (b, c, ODD_LORA_BLOCK)), sh_spec, st_spec,
                  vec(RW_COLS), vec(D), mat, vec(D), mat, mat, vec(D), vec(D), vec(D), vec(D), vec(D)],
        out_specs=[pl.BlockSpec((1, L, D), lambda b, c: (b, c, 0)), sh_spec, st_spec],
        out_shape=[jax.ShapeDtypeStruct((B, T, D), F32),
                   jax.ShapeDtypeStruct((B, SUBLANES, RW_COLS), F32),
                   jax.ShapeDtypeStruct((B, H, N, N), F32)],
        scratch_shapes=[pltpu.VMEM((SUBLANES + L, RW_COLS), F32), pltpu.VMEM((H // 2, N, 2 * N), F32),
                        rows, rows, rows, rows, rows, cols, cols, pltpu.VMEM((L, D), F32)],
        compiler_params=_cparams(("arbitrary", "arbitrary"), 48),
        name="rwkv_mixer",
    )(proj, proj, shift0, rw_state, mu.reshape(1, RW_COLS), w0.reshape(1, D), w2p, a0.reshape(1, D), a2p, g2,
      k_k.reshape(1, D), k_a.reshape(1, D), r_k.reshape(1, D), ln_g.reshape(1, D), ln_b.reshape(1, D))
    return y, shifto[:, SUBLANES - 1], sto


def _logf_kernel(fl_ref, bias_ref, o_ref):
    x = fl_ref[0] + bias_ref[...]
    lf = jnp.minimum(x, 0.0) - jnp.log1p(jnp.exp(-jnp.abs(x)))
    o_ref[0] = lf[:, 0:FOX_HEADS]


def fox_logf(proj, f_bias):
    B, T, _ = proj.shape
    tt = min(T, 512)
    return pl.pallas_call(
        _logf_kernel,
        grid=(B, T // tt),
        in_specs=[pl.BlockSpec((1, tt, LANES), lambda b, i: (b, i, ODD_FL_BLOCK)),
                  pl.BlockSpec((1, LANES), lambda b, i: (0, 0))],
        out_specs=pl.BlockSpec((1, tt, FOX_HEADS), lambda b, i: (b, i, 0)),
        out_shape=jax.ShapeDtypeStruct((B, T, FOX_HEADS), F32),
        compiler_params=_cparams(("arbitrary", "arbitrary")),
        name="fox_logf",
    )(proj, jnp.pad(f_bias.reshape(1, FOX_HEADS), ((0, 0), (0, LANES - FOX_HEADS))))


def _cumsum_kernel(x_ref, o_ref, carry_ref):
    ts = x_ref.shape[1]

    @pl.when(pl.program_id(1) == 0)
    def _():
        carry_ref[...] = jnp.zeros(carry_ref.shape, F32)

    row = lax.broadcasted_iota(jnp.int32, (ts, ts), 0)
    col = lax.broadcasted_iota(jnp.int32, (ts, ts), 1)
    cum = _mm_hi((col <= row).astype(F32), x_ref[0]) + carry_ref[...]
    o_ref[0] = cum
    carry_ref[...] = cum[ts - 1:ts, :]


def time_cumsum(x):
    B, S, H = x.shape
    ts = 512 if S % 512 == 0 else LANES
    assert S % ts == 0
    return pl.pallas_call(
        _cumsum_kernel,
        grid=(B, S // ts),
        in_specs=[pl.BlockSpec((1, ts, H), lambda b, i: (b, i, 0))],
        out_specs=pl.BlockSpec((1, ts, H), lambda b, i: (b, i, 0)),
        out_shape=jax.ShapeDtypeStruct((B, S, H), F32),
        scratch_shapes=[pltpu.VMEM((1, H), F32)],
        compiler_params=_cparams(("arbitrary", "arbitrary")),
        name="time_cumsum",
    )(x)


def _aug_placement(mode):
    h = jnp.arange(LANES)[:, None]
    c = jnp.arange(FOX_AW)[None, :]
    base = h * LANES + FOX_HD
    valid = h < FOX_HEADS
    term_off = 0 if mode == 'q' else 3
    sign = 1.0 if mode == 'q' else -1.0
    place = jnp.stack([jnp.where(valid & (c == base + term_off + i), sign, 0.0) for i in range(3)]).astype(BF16)
    lane = jnp.arange(FOX_AW) % LANES
    if mode == 'q':
        ones = (lane >= FOX_HD + 3) & (lane < FOX_HD + 6)
    elif mode == 'k':
        ones = (lane >= FOX_HD) & (lane < FOX_HD + 3)
    else:
        ones = lane == FOX_HD
    return place, jnp.where(ones, 1.0, 0.0).reshape(1, FOX_AW).astype(F32)


def _fox_aug_kernel(x_ref, cum_ref, place_ref, ones_ref, o_ref, *, mode):
    tt = x_ref.shape[1]
    x = x_ref[0]
    if mode == 'q':
        x = x * (FOX_HD ** -0.5 * LOG2E)
    aug = ones_ref[...]
    if mode != 'v':
        c = cum_ref[0] * LOG2E
        c1 = c.astype(BF16)
        r1 = c - c1.astype(F32)
        c2 = r1.astype(BF16)
        c3 = (r1 - c2.astype(F32)).astype(BF16)
        aug = aug + (jnp.dot(c1, place_ref[0], preferred_element_type=F32)
                     + jnp.dot(c2, place_ref[1], preferred_element_type=F32)
                     + jnp.dot(c3, place_ref[2], preferred_element_type=F32))
    else:
        aug = jnp.broadcast_to(aug, (tt, FOX_AW))
    lo = lax.broadcasted_iota(jnp.int32, (tt, LANES), 1) < FOX_HD
    for m in range(FOX_HEADS // 2):
        x_m = x[:, LANES * m:LANES * (m + 1)]
        even = jnp.where(lo, x_m, aug[:, 2 * LANES * m:2 * LANES * m + LANES])
        odd = jnp.where(lo, pltpu.roll(x_m, FOX_HD, 1), aug[:, 2 * LANES * m + LANES:2 * LANES * (m + 1)])
        o_ref[0, :, 2 * LANES * m:2 * LANES * (m + 1)] = jnp.concatenate([even, odd], axis=1).astype(BF16)


def fox_augment(arr, col_block, cum, mode):
    B, S, _ = arr.shape
    tt = min(S, 256)
    if S % tt:
        tt = LANES
    place, ones = _aug_placement(mode)
    out_spec = pl.BlockSpec((1, tt, FOX_AW), lambda b, i: (b, i, 0))
    out_shape = jax.ShapeDtypeStruct((B, S, FOX_AW), BF16)
    return pl.pallas_call(
        functools.partial(_fox_aug_kernel, mode=mode),
        grid=(B, S // tt),
        in_specs=[pl.BlockSpec((1, tt, D), lambda b, i: (b, i, col_block)),
                  pl.BlockSpec((1, tt, LANES), lambda b, i: (b, i, 0)),
                  pl.BlockSpec((3, LANES, FOX_AW), lambda b, i: (0, 0, 0)),
                  pl.BlockSpec((1, FOX_AW), lambda b, i: (0, 0))],
        out_specs=out_spec,
        out_shape=out_shape,
        compiler_params=_cparams(("arbitrary", "arbitrary")),
        name="fox_augment_" + mode,
    )(arr, cum, place, ones)


def _fox_kernel(i_ref, j_ref, qa_ref, ka_ref, va_ref, o_ref, m_ref, acc_ref, *, past, tq, tk):
    s_idx = pl.program_id(2)
    i = i_ref[s_idx]
    j = j_ref[s_idx]
    first_q = past + i * tq

    @pl.when(j == 0)
    def _():
        m_ref[...] = jnp.full(m_ref.shape, NEG_INF, F32)
        acc_ref[...] = jnp.zeros(acc_ref.shape, F32)

    def tile(masked):
        scores = []
        for hh in range(2):
            hs = slice(LANES * hh, LANES * (hh + 1))
            s = lax.dot_general(qa_ref[0, :, hs], ka_ref[0, :, hs], (((1,), (1,)), ((), ())),
                                preferred_element_type=F32)
            if masked:
                diff = (lax.broadcasted_iota(jnp.int32, (tq, tk), 1)
                        - lax.broadcasted_iota(jnp.int32, (tq, tk), 0))
                s = jnp.where(diff <= first_q - j * tk, s, NEG_INF)
            scores.append(s)
        for hh, s in enumerate(scores):
            hs = slice(LANES * hh, LANES * (hh + 1))
            chunks = [s[:, LANES * c:LANES * (c + 1)] for c in range(tk // LANES)]
            smax = functools.reduce(jnp.maximum, chunks)
            m_old = m_ref[hh]
            m_new = jnp.maximum(m_old, jnp.max(smax, axis=1, keepdims=True))
            pm = jnp.concatenate([jnp.exp2(ch - m_new) for ch in chunks], axis=1)
            acc_ref[hh] = (jnp.exp2(m_old - m_new) * acc_ref[hh]
                           + jnp.dot(pm.astype(BF16), va_ref[0, :, hs], preferred_element_type=F32))
            m_ref[hh] = m_new

    unmasked = (j + 1) * tk - 1 <= first_q

    @pl.when(unmasked)
    def _():
        tile(False)

    @pl.when(jnp.logical_not(unmasked))
    def _():
        tile(True)

    @pl.when(j == (first_q + tq - 1) // tk)
    def _():
        lo = lax.broadcasted_iota(jnp.int32, (tq, LANES), 1) < FOX_HD
        a0 = acc_ref[0]
        a1 = acc_ref[1]
        o0 = a0 / a0[:, FOX_HD:FOX_HD + 1]
        o1 = a1 / a1[:, FOX_HD:FOX_HD + 1]
        o_ref[0] = jnp.where(lo, o0, pltpu.roll(o1, FOX_HD, 1))


def fox_attention(qa, ka, va, past, T):
    B, S, _ = ka.shape
    tq = min(T, FOX_TQ)
    tk = FOX_TK if S % FOX_TK == 0 else S
    nq = T // tq
    pairs = [(i, j) for i in range(nq) for j in range((past + (i + 1) * tq - 1) // tk + 1)]
    i_idx = jnp.asarray([p[0] for p in pairs], jnp.int32)
    j_idx = jnp.asarray([p[1] for p in pairs], jnp.int32)
    grid_spec = pltpu.PrefetchScalarGridSpec(
        num_scalar_prefetch=2,
        grid=(B, FOX_HEADS // 2, len(pairs)),
        in_specs=[
            pl.BlockSpec((1, tq, 2 * LANES), lambda b, p, s, ii, jj: (b, ii[s], p)),
            pl.BlockSpec((1, tk, 2 * LANES), lambda b, p, s, ii, jj: (b, jj[s], p)),
            pl.BlockSpec((1, tk, 2 * LANES), lambda b, p, s, ii, jj: (b, jj[s], p)),
        ],
        out_specs=pl.BlockSpec((1, tq, LANES), lambda b, p, s, ii, jj: (b, ii[s], p)),
        scratch_shapes=[pltpu.VMEM((2, tq, LANES), F32), pltpu.VMEM((2, tq, LANES), F32)],
    )
    return pl.pallas_call(
        functools.partial(_fox_kernel, past=past, tq=tq, tk=tk),
        grid_spec=grid_spec,
        out_shape=jax.ShapeDtypeStruct((B, T, D), F32),
        compiler_params=_cparams(("arbitrary", "arbitrary", "arbitrary"), 48),
        name="fox_attention",
    )(i_idx, j_idx, qa, ka, va)


def _trunk(x, mods, prm, wts, conv_st, ssd_st, hg_st, shift_st, rw_st, fk_past, fv_past, fl_past):
    B, T, _ = x.shape
    n = B * T
    sh1, sc1, gt1, sh2, sc2, gt2 = mods[0]
    proj = norm_inproj(x, prm['norm_mix_g'][0], sc1, sh1, wts['even_in'])
    ya, n_conv, n_ssd = ssd_mixer(proj, conv_st, ssd_st, prm['ssd_conv_w'][0], prm['ssd_conv_b'][0],
                                  prm['ssd_dt_bias'][0], prm['ssd_a_log'][0], prm['ssd_d'][0],
                                  prm['ssd_norm_g'][0])
    yb, n_hg = hgrn_mixer(proj, hg_st, wts['hgrn_lb'], prm['hgrn_norm_g'][0])
    x, hn = mix_outproj(ya, yb, wts['even_out'], x, gt1, prm['norm_ffn_g'][0], sc2, sh2)
    hn2 = hn.reshape(n, D)
    gates = moe_router(hn2, prm['moe_router_w'][0], prm['moe_router_bias'][0])
    ffn = moe_experts(hn2, gates, 0, prm['moe_w_gate'], prm['moe_w_up'], prm['moe_w_down'],
                      prm['moe_sh_gate'], prm['moe_sh_up'], prm['moe_sh_down'])
    x = residual(x, ffn, gt2, prm['final_norm_g'], False)

    sh1, sc1, gt1, sh2, sc2, gt2 = mods[1]
    proj = norm_inproj(x, prm['norm_mix_g'][1], sc1, sh1, wts['odd_in'])
    yc, n_shift, n_rw = rwkv_mixer(proj, shift_st, rw_st, prm['rwkv_mu'][0], prm['rwkv_w0'][0], prm['rwkv_w2'][0],
                                   prm['rwkv_a0'][0], prm['rwkv_a2'][0], prm['rwkv_g2'][0], prm['rwkv_k_k'][0],
                                   prm['rwkv_k_a'][0], prm['rwkv_r_k'][0], prm['rwkv_ln_g'][0],
                                   prm['rwkv_ln_b'][0])
    logf = fox_logf(proj, prm['fox_f_bias'][0])
    kf = proj[:, :, ODD_K_BLOCK * D:(ODD_K_BLOCK + 1) * D]
    vf = proj[:, :, ODD_V_BLOCK * D:(ODD_V_BLOCK + 1) * D]
    lane_pad = lambda a: jnp.pad(a, ((0, 0), (0, 0), (0, LANES - FOX_HEADS)))
    if fk_past is None:
        past = 0
        cum = time_cumsum(lane_pad(logf))
        ka = fox_augment(proj, ODD_K_BLOCK, cum, 'k')
        va = fox_augment(proj, ODD_V_BLOCK, cum, 'v')
    else:
        past = fk_past.shape[1]
        pad = (-(past + T)) % LANES
        lf_all = jnp.concatenate([fl_past, logf, jnp.zeros((B, pad, FOX_HEADS), F32)], axis=1)
        cum = time_cumsum(lane_pad(lf_all))
        k_arr = jnp.concatenate([fk_past.reshape(B, past, D), kf, jnp.zeros((B, pad, D), F32)], axis=1)
        v_arr = jnp.concatenate([fv_past.reshape(B, past, D), vf, jnp.zeros((B, pad, D), F32)], axis=1)
        ka = fox_augment(k_arr, 0, cum, 'k')
        va = fox_augment(v_arr, 0, cum, 'v')
    qa = fox_augment(proj, ODD_Q_BLOCK, cum[:, past:past + T], 'q')
    yd = fox_attention(qa, ka, va, past, T)
    x, hn = mix_outproj(yc, yd, wts['odd_out'], x, gt1, prm['norm_ffn_g'][1], sc2, sh2)
    hn2 = hn.reshape(n, D)
    gates = moe_router(hn2, prm['moe_router_w'][1], prm['moe_router_bias'][1])
    ffn = moe_experts(hn2, gates, 1, prm['moe_w_gate'], prm['moe_w_up'], prm['moe_w_down'],
                      prm['moe_sh_gate'], prm['moe_sh_up'], prm['moe_sh_down'])
    y = residual(x, ffn, gt2, prm['final_norm_g'], True)
    hs = (B, T, FOX_HEADS, FOX_HD)
    return (y, n_conv[None], n_ssd[None], n_hg[None], n_shift[None], n_rw[None],
            kf.reshape(hs)[None], vf.reshape(hs)[None], logf[None])


def _prep_weights(prm):
    ein = prm['even_in_w'][0]
    z, xbc, dt, hq, hf, hi, hgate = jnp.split(ein, [1024, 2304, 2320, 3344, 4368, 5392], axis=1)
    even_in = jnp.concatenate([z, hq, hf, hi, hgate, xbc, dt, jnp.zeros((D, LANES - SSD_HEADS), F32)], axis=1)
    rkv, lora, qkv, fl = jnp.split(prm['odd_in_w'][0], [3 * D, RW_COLS, RW_COLS + 3 * D], axis=1)
    odd_in = jnp.concatenate([rkv, qkv, lora, fl, jnp.zeros((D, LANES - FOX_HEADS), F32)], axis=1)
    lb = jnp.cumsum(jax.nn.softmax(prm['hgrn_lb_logits'].astype(F32), axis=0), axis=0)[0]
    return {
        'even_in': even_in.astype(BF16),
        'odd_in': odd_in.astype(BF16),
        'even_out': prm['even_out_w'][0].astype(BF16),
        'odd_out': prm['odd_out_w'][0].astype(BF16),
        'hgrn_lb': lb,
    }


def kernel(x_prompt, x_sample, c_prompt, c_sample, state_ssd_conv, state_ssd, state_hgrn, state_rwkv_shift, state_rwkv, cache_fox_k, cache_fox_v, cache_fox_logf, ada_w, ada_b, norm_mix_g, norm_ffn_g, final_norm_g, even_in_w, even_out_w, ssd_conv_w, ssd_conv_b, ssd_dt_bias, ssd_a_log, ssd_d, ssd_norm_g, hgrn_lb_logits, hgrn_norm_g, odd_in_w, odd_out_w, rwkv_mu, rwkv_w0, rwkv_w2, rwkv_a0, rwkv_a2, rwkv_g2, rwkv_k_k, rwkv_k_a, rwkv_r_k, rwkv_ln_g, rwkv_ln_b, fox_f_bias, moe_router_w, moe_router_bias, moe_w_gate, moe_w_up, moe_w_down, moe_sh_gate, moe_sh_up, moe_sh_down):
    prm = dict(ada_w=ada_w, ada_b=ada_b, norm_mix_g=norm_mix_g, norm_ffn_g=norm_ffn_g, final_norm_g=final_norm_g,
               even_in_w=even_in_w, even_out_w=even_out_w, ssd_conv_w=ssd_conv_w, ssd_conv_b=ssd_conv_b,
               ssd_dt_bias=ssd_dt_bias, ssd_a_log=ssd_a_log, ssd_d=ssd_d, ssd_norm_g=ssd_norm_g,
               hgrn_lb_logits=hgrn_lb_logits, hgrn_norm_g=hgrn_norm_g,
               odd_in_w=odd_in_w, odd_out_w=odd_out_w, rwkv_mu=rwkv_mu, rwkv_w0=rwkv_w0, rwkv_w2=rwkv_w2,
               rwkv_a0=rwkv_a0, rwkv_a2=rwkv_a2, rwkv_g2=rwkv_g2, rwkv_k_k=rwkv_k_k, rwkv_k_a=rwkv_k_a,
               rwkv_r_k=rwkv_r_k, rwkv_ln_g=rwkv_ln_g, rwkv_ln_b=rwkv_ln_b, fox_f_bias=fox_f_bias,
               moe_router_w=moe_router_w, moe_router_bias=moe_router_bias, moe_w_gate=moe_w_gate,
               moe_w_up=moe_w_up, moe_w_down=moe_w_down, moe_sh_gate=moe_sh_gate, moe_sh_up=moe_sh_up,
               moe_sh_down=moe_sh_down)
    wts = _prep_weights(prm)
    bp, bs = x_prompt.shape[0], x_sample.shape[0]
    depth = ada_w.shape[0]
    c_all = jnp.concatenate([c_prompt, c_sample], axis=0)
    rows = c_all.shape[0]
    c_all = jnp.pad(c_all, ((0, (-rows) % SUBLANES), (0, 0)))
    mod = ada_mod(c_all, ada_w, ada_b)

    def mods_for(lo, hi):
        return [tuple(m[:, None, :] for m in jnp.split(mod[l, lo:hi], 6, axis=-1)) for l in range(depth)]

    z = lambda *s: jnp.zeros(s, F32)
    out_p = _trunk(x_prompt, mods_for(0, bp), prm, wts,
                   z(bp, SSD_CONV - 1, SSD_CONV_DIM), z(bp, SSD_HEADS, SSD_HD, SSD_STATE),
                   z(bp, HG_HEADS, HG_DK, HG_DV), z(bp, RW_COLS), z(bp, RW_HEADS, RW_N, RW_N),
                   None, None, None)
    out_s = _trunk(x_sample, mods_for(bp, bp + bs), prm, wts,
                   state_ssd_conv[0], state_ssd[0], state_hgrn[0], state_rwkv_shift[0], state_rwkv[0],
                   cache_fox_k[0], cache_fox_v[0], cache_fox_logf[0])
    return (out_p[0], out_s[0]) + tuple(out_p[1:]) + tuple(out_s[1:])
```

```python
import functools

import jax
import jax.numpy as jnp
from jax import lax
from jax.experimental import pallas as pl
from jax.experimental.pallas import tpu as pltpu

F32 = jnp.float32
BF16 = jnp.bfloat16
HI = lax.Precision.HIGHEST

D = 1024
EPS = 1e-6
LANES = 128
SUBLANES = 8

SSD_HEADS = 16
SSD_HD = 64
SSD_STATE = 64
SSD_CONV = 4
SSD_CONV_DIM = 1280
SSD_L = 64
HG_HEADS = 8
HG_DK = 128
HG_DV = 128
HG_L = 64
RW_HEADS = 16
RW_N = 64
RW_COLS = 3328
RW_LN_EPS = 64e-5
RW_L = 64
FOX_HEADS = 16
FOX_HD = 64
FOX_AW = FOX_HEADS * LANES
FOX_TQ = 1024
FOX_TK = 1024
FOX_KC = 512
LOG2E = 1.4426950408889634
ODD_Q_BLOCK, ODD_K_BLOCK, ODD_V_BLOCK = 3, 4, 5
ODD_LORA_BLOCK = 24
ODD_FL_BLOCK = 50
RW_LORA = 256
N_EXPERTS = 64
TOP_K = 6
N_GROUPS = 8
TOPK_GROUPS = 4
D_EXPERT = 256
ROUTED_SCALE = 2.5
PROJ_COLS = 6528
PROJ_TN = 2176

NEG_INF = float("-inf")


def _cparams(sem, vmem_mb=None):
    kw = dict(dimension_semantics=sem)
    if vmem_mb is not None:
        kw["vmem_limit_bytes"] = vmem_mb * 1024 * 1024
    return pltpu.CompilerParams(**kw)


def _mm(a, b):
    return jnp.dot(a.astype(BF16), b.astype(BF16), preferred_element_type=F32)


def _mm_nt(a, b):
    return lax.dot_general(a.astype(BF16), b.astype(BF16), (((1,), (1,)), ((), ())),
                           preferred_element_type=F32)


def _mm_hi(a, b):
    return jnp.dot(a, b, precision=HI, preferred_element_type=F32)


def _mm_01(sel_bf, x):
    x1 = x.astype(BF16)
    r1 = x - x1.astype(F32)
    x2 = r1.astype(BF16)
    x3 = (r1 - x2.astype(F32)).astype(BF16)
    return (jnp.dot(sel_bf, x1, preferred_element_type=F32) + jnp.dot(sel_bf, x2, preferred_element_type=F32)
            + jnp.dot(sel_bf, x3, preferred_element_type=F32))


def _sigmoid(x):
    return 1.0 / (1.0 + jnp.exp(-x))


def _silu(x):
    return x * _sigmoid(x)


def _softplus(x):
    return jnp.maximum(x, 0.0) + jnp.log1p(jnp.exp(-jnp.abs(x)))


def _row_tiles(G, gs, target):
    if gs >= target:
        assert gs % target == 0
        return 1, target
    gb = max(1, min(G, target // gs))
    while G % gb:
        gb -= 1
    return gb, gs


def _ada_kernel(c_ref, w_ref, b_ref, o_ref):
    c = c_ref[...]
    o_ref[0] = _mm(_silu(c), w_ref[0]) + b_ref[0]


def ada_mod(c_all, ada_w, ada_b):
    depth = ada_w.shape[0]
    bp = c_all.shape[0]
    tn = 1024
    return pl.pallas_call(
        _ada_kernel,
        grid=(depth, 6 * D // tn),
        in_specs=[
            pl.BlockSpec((bp, D), lambda l, j: (0, 0)),
            pl.BlockSpec((1, D, tn), lambda l, j: (l, 0, j)),
            pl.BlockSpec((1, 1, tn), lambda l, j: (l, 0, j)),
        ],
        out_specs=pl.BlockSpec((1, bp, tn), lambda l, j: (l, 0, j)),
        out_shape=jax.ShapeDtypeStruct((depth, bp, 6 * D), F32),
        compiler_params=_cparams(("arbitrary", "arbitrary")),
        name="ada_mod",
    )(c_all, ada_w, ada_b.reshape(depth, 1, 6 * D))


def _rms_mod(x, g, sc, sh):
    y = x * lax.rsqrt(jnp.mean(x * x, axis=-1, keepdims=True) + EPS)
    return (y * g) * (1.0 + sc) + sh


def _inproj_kernel(x_ref, g_ref, sc_ref, sh_ref, w_ref, o_ref, xn_ref):
    @pl.when(pl.program_id(1) == 0)
    def _():
        hn = _rms_mod(x_ref[...], g_ref[...], sc_ref[...], sh_ref[...])
        xn_ref[...] = hn.reshape(xn_ref.shape).astype(BF16)

    o_ref[...] = jnp.dot(xn_ref[...], w_ref[...], preferred_element_type=F32)


def norm_inproj(x, g, sc, sh, w_bf):
    G, gs, _ = x.shape
    gb, tb = _row_tiles(G, gs, 512)
    nt = gs // tb
    tt = gb * tb
    n = G * gs
    out = pl.pallas_call(
        _inproj_kernel,
        grid=(n // tt, PROJ_COLS // PROJ_TN),
        in_specs=[
            pl.BlockSpec((gb, tb, D), lambda i, j: (i // nt, i % nt, 0)),
            pl.BlockSpec((1, D), lambda i, j: (0, 0)),
            pl.BlockSpec((gb, 1, D), lambda i, j: (i // nt, 0, 0)),
            pl.BlockSpec((gb, 1, D), lambda i, j: (i // nt, 0, 0)),
            pl.BlockSpec((D, PROJ_TN), lambda i, j: (0, j)),
        ],
        out_specs=pl.BlockSpec((tt, PROJ_TN), lambda i, j: (i, j)),
        out_shape=jax.ShapeDtypeStruct((n, PROJ_COLS), F32),
        scratch_shapes=[pltpu.VMEM((tt, D), BF16)],
        compiler_params=_cparams(("arbitrary", "arbitrary"), 48),
        name="norm_inproj",
    )(x, g.reshape(1, D), sc, sh, w_bf)
    return out.reshape(G, gs, PROJ_COLS)


def _outproj_kernel(ya_ref, yb_ref, w_ref, x_ref, gt_ref, g_ref, sc_ref, sh_ref, xo_ref, hn_ref):
    gb, tb, _ = x_ref.shape
    out = (jnp.dot(ya_ref[...].astype(BF16), w_ref[0:D, :], preferred_element_type=F32)
           + jnp.dot(yb_ref[...].astype(BF16), w_ref[D:2 * D, :], preferred_element_type=F32))
    x = x_ref[...] + gt_ref[...] * out.reshape(gb, tb, D)
    xo_ref[...] = x
    hn_ref[...] = _rms_mod(x, g_ref[...], sc_ref[...], sh_ref[...])


def mix_outproj(ya, yb, w_bf, x, gt, g, sc, sh):
    G, gs, _ = x.shape
    gb, tb = _row_tiles(G, gs, 512)
    nt = gs // tb
    tt = gb * tb
    n = G * gs
    row3 = pl.BlockSpec((gb, tb, D), lambda i: (i // nt, i % nt, 0))
    mod3 = pl.BlockSpec((gb, 1, D), lambda i: (i // nt, 0, 0))
    return pl.pallas_call(
        _outproj_kernel,
        grid=(n // tt,),
        in_specs=[
            pl.BlockSpec((tt, D), lambda i: (i, 0)),
            pl.BlockSpec((tt, D), lambda i: (i, 0)),
            pl.BlockSpec((2 * D, D), lambda i: (0, 0)),
            row3, mod3,
            pl.BlockSpec((1, D), lambda i: (0, 0)),
            mod3, mod3,
        ],
        out_specs=[row3, row3],
        out_shape=[jax.ShapeDtypeStruct((G, gs, D), F32), jax.ShapeDtypeStruct((G, gs, D), F32)],
        compiler_params=_cparams(("arbitrary",), 48),
        name="mix_outproj",
    )(ya.reshape(n, D), yb.reshape(n, D), w_bf, x, gt, g.reshape(1, D), sc, sh)


def _ssd_kernel(z_ref, xbc_ref, dt_ref, conv0_ref, st0_ref, cw_ref, cb_ref, dtb_ref, alog_ref, dfull_ref,
                ng_ref, y_ref, convo_ref, sto_ref, ext_ref, st_ref, yacc_ref):
    L = SSD_L
    c = pl.program_id(1)

    bd_r = lax.broadcasted_iota(jnp.int32, (LANES, LANES), 0) // SSD_HD
    bd_c = lax.broadcasted_iota(jnp.int32, (LANES, LANES), 1) // SSD_HD
    blockdiag = bd_r == bd_c

    @pl.when(c == 0)
    def _():
        ext_ref[0:SUBLANES, :] = conv0_ref[0]
        for p in range(SSD_HEADS // 2):
            x = st0_ref[0, p]
            st_ref[p] = jnp.where(blockdiag, jnp.concatenate([x, x], axis=1), 0.0)

    u = xbc_ref[0]
    ext_ref[SUBLANES:SUBLANES + L, :] = u
    acc = cb_ref[...] + u * cw_ref[SSD_CONV - 1:SSD_CONV, :]
    for i in range(SSD_CONV - 1):
        sh = SSD_CONV - 1 - i
        acc = acc + ext_ref[SUBLANES - sh:SUBLANES - sh + L, :] * cw_ref[i:i + 1, :]
    ext_ref[0:SUBLANES, :] = ext_ref[L:L + SUBLANES, :]
    xbc = _silu(acc)
    xs = xbc[:, 0:D]
    bm2 = xbc[:, D:D + LANES]
    cm2 = xbc[:, D + LANES:D + 2 * LANES]

    dt = _softplus(dt_ref[0] + dtb_ref[...])
    a = dt * (-jnp.exp(alog_ref[...]))
    row = lax.broadcasted_iota(jnp.int32, (L, L), 0)
    col = lax.broadcasted_iota(jnp.int32, (L, L), 1)
    tril = (col <= row).astype(F32)
    triu = (row <= col).astype(F32)
    cum = _mm_hi(tril, a)
    cum_t = _mm_hi(a.T, triu)
    eh = lax.broadcasted_iota(jnp.int32, (LANES, D), 0)
    ec = lax.broadcasted_iota(jnp.int32, (LANES, D), 1)
    expand = (ec // SSD_HD == eh).astype(F32)
    dt_full = _mm_hi(dt, expand)
    cum_full = _mm_hi(cum, expand)
    ecum_full = jnp.exp(cum_full)
    last_full = cum_full[L - 1:L, :]
    w_full = jnp.exp(last_full - cum_full)
    xdt = xs * dt_full

    lane = lax.broadcasted_iota(jnp.int32, (L, LANES), 1)
    lo = lane < SSD_HD
    tri2 = (lane % L) <= lax.broadcasted_iota(jnp.int32, (L, LANES), 0)
    bm_r = pltpu.roll(bm2, SSD_HD, 1)
    cm_r = pltpu.roll(cm2, SSD_HD, 1)
    bmstack = jnp.concatenate([bm2, bm2], axis=0)

    for g in range(2):
        grp_lo = lo if g == 0 else jnp.logical_not(lo)
        cm_m = jnp.where(grp_lo, cm2, 0.0)
        cb_dup = _mm_nt(cm_m, bmstack)
        cm_dup = jnp.where(grp_lo, cm2, cm_r)
        bm_dup = jnp.where(grp_lo, bm2, bm_r)
        for pp in range(4):
            p = 4 * g + pp
            sl = slice(LANES * p, LANES * (p + 1))
            cum_pair = cum_full[:, sl]
            cum_row = jnp.concatenate([cum_t[2 * p:2 * p + 1, :], cum_t[2 * p + 1:2 * p + 2, :]], axis=1)
            s2 = cb_dup * jnp.exp(jnp.where(tri2, cum_pair - cum_row, NEG_INF))
            x2 = xdt[:, sl]
            xstack = jnp.concatenate([jnp.where(lo, x2, 0.0), jnp.where(lo, 0.0, x2)], axis=0)
            hp = st_ref[p]
            lhs = jnp.concatenate([s2, cm_dup * ecum_full[:, sl]], axis=1)
            rhs = jnp.concatenate([xstack, hp], axis=0)
            yacc_ref[:, sl] = _mm(lhs, rhs)
            bw_t = (bm_dup * w_full[:, sl]).T
            upd = _mm(bw_t, x2)
            st_ref[p] = jnp.where(blockdiag, hp * ecum_full[L - 1:L, sl] + upd, 0.0)

    y = yacc_ref[...] + xs * dfull_ref[...]
    z = z_ref[0]
    yz = y * _silu(z)
    y_ref[0] = yz * lax.rsqrt(jnp.mean(yz * yz, axis=-1, keepdims=True) + EPS) * ng_ref[...]

    @pl.when(c == pl.num_programs(1) - 1)
    def _():
        convo_ref[0] = ext_ref[0:SUBLANES, :]
        first = lax.broadcasted_iota(jnp.int32, (LANES, SSD_HD), 0) < SSD_STATE
        for p in range(SSD_HEADS // 2):
            hp = st_ref[p]
            sto_ref[0, p] = jnp.where(first, hp[:, 0:SSD_HD], hp[:, SSD_HD:2 * SSD_HD])


def ssd_mixer(proj, conv_state, ssd_state, conv_w, conv_b, dt_bias, a_log, d_skip, norm_g):
    B, T, _ = proj.shape
    L = SSD_L
    assert T % L == 0
    H, P, N = SSD_HEADS, SSD_HD, SSD_STATE
    conv0 = jnp.pad(conv_state, ((0, 0), (SUBLANES - (SSD_CONV - 1), 0), (0, 0)))
    st0 = jnp.swapaxes(ssd_state, -1, -2).reshape(B, H // 2, 2 * N, P)
    st_spec = pl.BlockSpec((1, H // 2, 2 * N, P), lambda b, c: (b, 0, 0, 0))
    cw =jnp.pad(conv_w, ((0, SUBLANES - SSD_CONV), (0, 0)))
    pad16 = lambda v: jnp.pad(v.reshape(1, H), ((0, 0), (0, LANES - H)))
    dfull = jnp.repeat(d_skip, P).reshape(1, D)
    vec = lambda w: pl.BlockSpec((1, w), lambda b, c: (0, 0))
    y, convo, sto = pl.pallas_call(
        _ssd_kernel,
        grid=(B, T // L),
        in_specs=[
            pl.BlockSpec((1, L, D), lambda b, c: (b, c, 0)),
            pl.BlockSpec((1, L, SSD_CONV_DIM), lambda b, c: (b, c, 4)),
            pl.BlockSpec((1, L, LANES), lambda b, c: (b, c, 50)),
            pl.BlockSpec((1, SUBLANES, SSD_CONV_DIM), lambda b, c: (b, 0, 0)),
            st_spec,
            pl.BlockSpec((SUBLANES, SSD_CONV_DIM), lambda b, c: (0, 0)),
            vec(SSD_CONV_DIM), vec(LANES), vec(LANES), vec(D), vec(D),
        ],
        out_specs=[
            pl.BlockSpec((1, L, D), lambda b, c: (b, c, 0)),
            pl.BlockSpec((1, SUBLANES, SSD_CONV_DIM), lambda b, c: (b, 0, 0)),
            st_spec,
        ],
        out_shape=[
            jax.ShapeDtypeStruct((B, T, D), F32),
            jax.ShapeDtypeStruct((B, SUBLANES, SSD_CONV_DIM), F32),
            jax.ShapeDtypeStruct((B, H // 2, 2 * N, P), F32),
        ],
        scratch_shapes=[
            pltpu.VMEM((SUBLANES + L, SSD_CONV_DIM), F32),
            pltpu.VMEM((H // 2, LANES, LANES), F32),
            pltpu.VMEM((L, D), F32),
        ],
        compiler_params=_cparams(("arbitrary", "arbitrary"), 48),
        name="ssd_mixer",
    )(proj, proj, proj, conv0, st0, cw, conv_b.reshape(1, SSD_CONV_DIM), pad16(dt_bias), pad16(a_log),
      dfull, norm_g.reshape(1, D))
    return y, convo[:, SUBLANES - (SSD_CONV - 1):], jnp.swapaxes(sto.reshape(B, H, N, P), -1, -2)


def _hgrn_kernel(q_ref, f_ref, i_ref, gate_ref, st0_ref, lb_ref, ng_ref, y_ref, sto_ref, st_ref):
    L = HG_L
    c = pl.program_id(1)

    @pl.when(c == 0)
    def _():
        st_ref[...] = st0_ref[0]

    lb = lb_ref[...]
    f = lb + (1.0 - lb) * _sigmoid(f_ref[0])
    logf = jnp.log(f)
    q = _silu(q_ref[0]) * (HG_DK ** -0.5)
    k = 1.0 - f
    v = i_ref[0]
    row = lax.broadcasted_iota(jnp.int32, (L, L), 0)
    col = lax.broadcasted_iota(jnp.int32, (L, L), 1)
    cum = _mm_01(jnp.where(col <= row, 1.0, 0.0).astype(BF16), logf)
    qe = q * jnp.exp(cum)
    last = cum[L - 1:L, :]
    kw = k * jnp.exp(last - cum)
    elast = jnp.exp(last)

    halves = [L >> (i + 1) for i in range(L.bit_length() - 1)]
    small = [hs for hs in halves if 2 * hs <= SUBLANES]
    rsel = lax.broadcasted_iota(jnp.int32, (L * len(small), L), 0)
    csel = lax.broadcasted_iota(jnp.int32, (L * len(small), L), 1)
    ref_of = jnp.zeros_like(rsel)
    for i, hs in enumerate(small):
        ref_of = jnp.where(rsel // L == i, ((rsel - L * i) // (2 * hs)) * (2 * hs) + hs - 1, ref_of)
    ref_small = _mm_01(jnp.where(csel == ref_of, 1.0, 0.0).astype(BF16), cum)
    rvec = lax.broadcasted_iota(jnp.int32, (L, 1), 0)
    qs, ks, pair_mask = [], [], []
    for hs in halves:
        if hs in small:
            i = small.index(hs)
            cum_ref = ref_small[L * i:L * (i + 1)]
        else:
            cum_ref = jnp.concatenate(
                [jnp.broadcast_to(cum[b + hs - 1:b + hs, :], (2 * hs, D)) for b in range(0, L, 2 * hs)], axis=0)
        upper = (rvec % (2 * hs)) >= hs
        d = cum - cum_ref
        e = jnp.exp(jnp.where(upper, d, -d))
        qs.append((q * e).astype(BF16))
        ks.append((k * e).astype(BF16))
        pair_mask.append(jnp.where((row // (2 * hs) == col // (2 * hs)) & (row % (2 * hs) >= hs)
                                   & (col % (2 * hs) < hs), 1.0, 0.0))
    eye = jnp.where(row == col, 1.0, 0.0)
    qb, kb = q.astype(BF16), k.astype(BF16)

    heads = [slice(HG_DK * h, HG_DK * (h + 1)) for h in range(HG_HEADS)]
    scores = []
    for sl in heads:
        s = eye * _mm_nt(qb[:, sl], kb[:, sl])
        for i in range(len(halves)):
            s = s + pair_mask[i] * _mm_nt(qs[i][:, sl], ks[i][:, sl])
        scores.append(s.astype(BF16))
    vb = v.astype(BF16)
    outs = []
    for h, sl in enumerate(heads):
        st = st_ref[h]
        outs.append(_mm_nt(qe[:, sl], st) + jnp.dot(scores[h], vb[:, sl], preferred_element_type=F32))
        st_ref[h] = st * elast[:, sl] + _mm(v[:, sl].T, kw[:, sl])
    for (h, sl), o in zip(enumerate(heads), outs):
        on = o * lax.rsqrt(jnp.mean(o * o, axis=-1, keepdims=True) + EPS) * ng_ref[:, sl]
        y_ref[0, :, sl] = on * _silu(gate_ref[0, :, sl])

    @pl.when(c == pl.num_programs(1) - 1)
    def _():
        sto_ref[0] = st_ref[...]


def hgrn_mixer(proj, hg_state, lb, norm_g):
    B, T, _ = proj.shape
    L = HG_L
    assert T % L == 0
    st0 = jnp.swapaxes(hg_state, -1, -2)
    col = lambda j: pl.BlockSpec((1, L, D), lambda b, c: (b, c, j))
    st_spec = pl.BlockSpec((1, HG_HEADS, HG_DV, HG_DK), lambda b, c: (b, 0, 0, 0))
    vec = pl.BlockSpec((1, D), lambda b, c: (0, 0))
    y, sto = pl.pallas_call(
        _hgrn_kernel,
        grid=(B, T // L),
        in_specs=[col(1), col(2), col(3), col(4), st_spec, vec, vec],
        out_specs=[pl.BlockSpec((1, L, D), lambda b, c: (b, c, 0)), st_spec],
        out_shape=[jax.ShapeDtypeStruct((B, T, D), F32),
                   jax.ShapeDtypeStruct((B, HG_HEADS, HG_DV, HG_DK), F32)],
        scratch_shapes=[pltpu.VMEM((HG_HEADS, HG_DV, HG_DK), F32)],
        compiler_params=_cparams(("arbitrary", "arbitrary")),
        name="hgrn_mixer",
    )(proj, proj, proj, proj, st0, lb.reshape(1, D), norm_g.reshape(1, D))
    return y, jnp.swapaxes(sto, -1, -2)


def _first_max_onehot(vals, iota, axis, size):
    m = jnp.max(vals, axis=axis, keepdims=True)
    idx = jnp.min(jnp.where(vals == m, iota, size), axis=axis, keepdims=True)
    return iota == idx, m


def _router_kernel(x_ref, wt_ref, bias_ref, o_ref):
    tt = x_ref.shape[0]
    per = N_EXPERTS // N_GROUPS
    logits = lax.dot_general(wt_ref[...], x_ref[...], (((1,), (1,)), ((), ())),
                             precision=HI, preferred_element_type=F32)
    scores = _sigmoid(logits)
    biased = scores + bias_ref[...]
    b3 = biased.reshape(N_GROUPS, per, tt)
    i3 = lax.broadcasted_iota(jnp.int32, (N_GROUPS, per, tt), 1)
    oh, m1 = _first_max_onehot(b3, i3, 1, per)
    m2 = jnp.max(jnp.where(oh, NEG_INF, b3), axis=1, keepdims=True)
    grp = (m1 + m2).reshape(N_GROUPS, tt)
    ig = lax.broadcasted_iota(jnp.int32, (N_GROUPS, tt), 0)
    gsel = jnp.zeros((N_GROUPS, tt), jnp.bool_)
    for _ in range(TOPK_GROUPS):
        oh, _m = _first_max_onehot(grp, ig, 0, N_GROUPS)
        gsel = jnp.logical_or(gsel, oh)
        grp = jnp.where(oh, NEG_INF, grp)
    gmask = jnp.where(gsel, 1.0, 0.0).reshape(N_GROUPS, 1, tt)
    masked = jnp.where(gmask > 0.0, b3, NEG_INF).reshape(N_EXPERTS, tt)
    ie = lax.broadcasted_iota(jnp.int32, (N_EXPERTS, tt), 0)
    esel = jnp.zeros((N_EXPERTS, tt), F32)
    for _ in range(TOP_K):
        oh, _m = _first_max_onehot(masked, ie, 0, N_EXPERTS)
        esel = jnp.where(oh, 1.0, esel)
        masked = jnp.where(oh, NEG_INF, masked)
    wsel = scores * esel
    gates = wsel / jnp.sum(wsel, axis=0, keepdims=True) * ROUTED_SCALE
    o_ref[...] = gates.T


def moe_router(hn2d, router_w, router_bias):
    n = hn2d.shape[0]
    tt = min(n, 512)
    return pl.pallas_call(
        _router_kernel,
        grid=(n // tt,),
        in_specs=[
            pl.BlockSpec((tt, D), lambda i: (i, 0)),
            pl.BlockSpec((N_EXPERTS, D), lambda i: (0, 0)),
            pl.BlockSpec((N_EXPERTS, 1), lambda i: (0, 0)),
        ],
        out_specs=pl.BlockSpec((tt, N_EXPERTS), lambda i: (i, 0)),
        out_shape=jax.ShapeDtypeStruct((n, N_EXPERTS), F32),
        compiler_params=_cparams(("arbitrary",)),
        name="moe_router",
    )(hn2d, router_w.T, router_bias.reshape(N_EXPERTS, 1))


def _ffn(xb, wg, wu, wd, gate_col):
    h = _silu(jnp.dot(xb, wg.astype(BF16), preferred_element_type=F32)) * jnp.dot(
        xb, wu.astype(BF16), preferred_element_type=F32)
    if gate_col is not None:
        h = h * gate_col
    return jnp.dot(h.astype(BF16), wd.astype(BF16), preferred_element_type=F32)


def _moe_kernel(x_ref, gates_ref, wg_ref, wu_ref, wd_ref, sg_ref, su_ref, sd_ref, o_ref, xb_ref, acc_ref):
    e = pl.program_id(1)

    @pl.when(e == 0)
    def _():
        xb_ref[...] = x_ref[...].astype(BF16)
        acc_ref[...] = _ffn(xb_ref[...], sg_ref[0], su_ref[0], sd_ref[0], None)

    gates = gates_ref[...]
    lane = lax.broadcasted_iota(jnp.int32, gates.shape, 1)
    gate_col = jnp.sum(jnp.where(lane == e, gates, 0.0), axis=1, keepdims=True)
    acc_ref[...] += _ffn(xb_ref[...], wg_ref[0, 0], wu_ref[0, 0], wd_ref[0, 0], gate_col)

    @pl.when(e == pl.num_programs(1) - 1)
    def _():
        o_ref[...] = acc_ref[...]


def moe_experts(hn2d, gates, l, w_gate, w_up, w_down, sh_gate, sh_up, sh_down):
    n = hn2d.shape[0]
    tt = min(n, 1024)
    return pl.pallas_call(
        _moe_kernel,
        grid=(n // tt, N_EXPERTS),
        in_specs=[
            pl.BlockSpec((tt, D), lambda i, e: (i, 0)),
            pl.BlockSpec((tt, N_EXPERTS), lambda i, e: (i, 0)),
            pl.BlockSpec((1, 1, D, D_EXPERT), lambda i, e: (l, e, 0, 0)),
            pl.BlockSpec((1, 1, D, D_EXPERT), lambda i, e: (l, e, 0, 0)),
            pl.BlockSpec((1, 1, D_EXPERT, D), lambda i, e: (l, e, 0, 0)),
            pl.BlockSpec((1, D, D_EXPERT), lambda i, e: (l, 0, 0)),
            pl.BlockSpec((1, D, D_EXPERT), lambda i, e: (l, 0, 0)),
            pl.BlockSpec((1, D_EXPERT, D), lambda i, e: (l, 0, 0)),
        ],
        out_specs=pl.BlockSpec((tt, D), lambda i, e: (i, 0)),
        out_shape=jax.ShapeDtypeStruct((n, D), F32),
        scratch_shapes=[pltpu.VMEM((tt, D), BF16), pltpu.VMEM((tt, D), F32)],
        compiler_params=_cparams(("arbitrary", "arbitrary"), 56),
        name="moe_experts",
    )(hn2d, gates, w_gate, w_up, w_down, sh_gate, sh_up, sh_down)


def _residual_kernel(x_ref, f_ref, gt_ref, g_ref, o_ref, *, final_norm):
    x = x_ref[...] + gt_ref[...] * f_ref[...]
    if final_norm:
        x = x * lax.rsqrt(jnp.mean(x * x, axis=-1, keepdims=True) + EPS) * g_ref[...]
    o_ref[...] = x


def residual(x, ffn, gt, g, final_norm):
    G, gs, _ = x.shape
    gb, tb = _row_tiles(G, gs, 512)
    nt = gs // tb
    row3 = pl.BlockSpec((gb, tb, D), lambda i: (i // nt, i % nt, 0))
    return pl.pallas_call(
        functools.partial(_residual_kernel, final_norm=final_norm),
        grid=(G * gs // (gb * tb),),
        in_specs=[row3, row3, pl.BlockSpec((gb, 1, D), lambda i: (i // nt, 0, 0)),
                  pl.BlockSpec((1, D), lambda i: (0, 0))],
        out_specs=row3,
        out_shape=jax.ShapeDtypeStruct((G, gs, D), F32),
        compiler_params=_cparams(("arbitrary",)),
        name="residual",
    )(x, ffn.reshape(G, gs, D), gt, g.reshape(1, D))


def _head_sum(x):
    L = x.shape[0]
    lo = lax.broadcasted_iota(jnp.int32, (L, LANES), 1) < RW_N
    outs = []
    for p in range(RW_HEADS // 2):
        x2 = x[:, LANES * p:LANES * (p + 1)]
        s_lo = jnp.sum(jnp.where(lo, x2, 0.0), axis=1, keepdims=True)
        s_hi = jnp.sum(jnp.where(lo, 0.0, x2), axis=1, keepdims=True)
        outs.append(jnp.where(lo, s_lo, s_hi))
    return jnp.concatenate(outs, axis=1)


def _rwkv_kernel(rkv_ref, lora_ref, shift0_ref, st0_ref, mu_ref, w0_ref, w2_ref, a0_ref, a2_ref, g2_ref, kk_ref, ka_ref,
                 rk_ref, lng_ref, lnb_ref, y_ref, shifto_ref, sto_ref,
                 ext_ref, st_ref, r_s, w_s, k_s, b_s, nkk_s, vt_s, ot_s, o_s):
    L = RW_L
    N = RW_N
    NP = RW_HEADS // 2
    c = pl.program_id(1)

    @pl.when(c == 0)
    def _():
        ext_ref[0:SUBLANES, :] = shift0_ref[0]
        for p in range(NP):
            st_ref[p] = jnp.concatenate([st0_ref[0, 2 * p], st0_ref[0, 2 * p + 1]], axis=1)

    pr = jnp.concatenate([rkv_ref[0], lora_ref[0]], axis=1)
    ext_ref[SUBLANES:SUBLANES + L, :] = pr
    prev = ext_ref[SUBLANES - 1:SUBLANES - 1 + L, :]
    ext_ref[0:SUBLANES, :] = ext_ref[L:L + SUBLANES, :]
    ps = pr + (prev - pr) * mu_ref[...]
    r = ps[:, 0:D]
    k = ps[:, D:2 * D]
    v = ps[:, 2 * D:3 * D]
    wa = ps[:, 3 * D:3 * D + LANES]
    gd = ps[:, 3 * D + LANES:3 * D + 2 * LANES]
    w = -_softplus(-(w0_ref[...] + _mm(jnp.tanh(wa), w2_ref[...]))) - 0.5
    wdec = jnp.exp(-jnp.exp(w))
    a = _sigmoid(a0_ref[...] + _mm(wa, a2_ref[...]))
    g = _mm(_sigmoid(gd), g2_ref[...])
    kk = k * kk_ref[...]
    kk = kk * lax.rsqrt(jnp.maximum(_head_sum(kk * kk), 1e-24))
    k = k * (1.0 + (a - 1.0) * ka_ref[...])
    bonus = _head_sum(r * k * rk_ref[...]) * v
    b = kk * a
    for p in range(NP):
        ps_ = slice(LANES * p, LANES * (p + 1))
        r_s[p] = r[:, ps_]
        w_s[p] = wdec[:, ps_]
        k_s[p] = k[:, ps_]
        b_s[p] = b[:, ps_]
        nkk_s[p] = -kk[:, ps_]
        v2 = v[:, ps_]
        vt_s[p] = jnp.concatenate([v2[:, 0:N], v2[:, N:2 * N]], axis=0).T
    ot_s[...] = jnp.zeros(ot_s.shape, F32)

    bd_r = lax.broadcasted_iota(jnp.int32, (2 * LANES, 2 * LANES), 0) // N
    bd_c = lax.broadcasted_iota(jnp.int32, (2 * LANES, 2 * LANES), 1) // N
    seg_ones = jnp.where(bd_r == bd_c, 1.0, 0.0).astype(BF16)

    def seg_sum(parts):
        half = len(parts) // 2
        outs = []
        for grp in (parts[:half], parts[half:]):
            x = jnp.concatenate([jnp.concatenate(grp[0::2], axis=0), jnp.concatenate(grp[1::2], axis=0)], axis=1)
            y = jnp.dot(x.astype(BF16), seg_ones, preferred_element_type=F32)
            for i in range(len(grp)):
                outs.append(y[N * (i // 2):N * (i // 2 + 1), LANES * (i % 2):LANES * (i % 2 + 1)])
        return outs

    lane_time = lax.broadcasted_iota(jnp.int32, (N, LANES), 1) % L
    row = lambda ref, p, t: ref[p, pl.ds(t, 1), :]

    def step(t, carry):
        tp = jnp.maximum(t - 1, 0)
        parts = []
        for p in range(NP):
            s = st_ref[p]
            parts += [s * row(nkk_s, p, t), jnp.where(lane_time == t, vt_s[p], 0.0), s * row(r_s, p, tp)]
        res = seg_sum(parts)
        for p in range(NP):
            sa, vcol, o_prev = res[3 * p:3 * p + 3]
            st_ref[p] = st_ref[p] * row(w_s, p, t) + sa * row(b_s, p, t) + vcol * row(k_s, p, t)
            ot_s[p] = jnp.where(lane_time == t - 1, o_prev, ot_s[p])
        return carry

    lax.fori_loop(0, L, step, 0, unroll=2)
    o_last = seg_sum([st_ref[p] * row(r_s, p, L - 1) for p in range(NP)])
    for p in range(NP):
        ot_s[p] = jnp.where(lane_time == L - 1, o_last[p], ot_s[p])

    for p in range(NP):
        ot = ot_s[p]
        mean = jnp.mean(ot, axis=0, keepdims=True)
        cen = ot - mean
        var = jnp.mean(cen * cen, axis=0, keepdims=True)
        on_t = (cen * lax.rsqrt(var + RW_LN_EPS)).T
        o_s[:, LANES * p:LANES * p + N] = on_t[0:L]
        o_s[:, LANES * p + N:LANES * (p + 1)] = on_t[L:2 * L]
    o = o_s[...] * lng_ref[...] + lnb_ref[...]
    y_ref[0] = (o + bonus) * g

    @pl.when(c == pl.num_programs(1) - 1)
    def _():
        shifto_ref[0] = ext_ref[0:SUBLANES, :]
        for p in range(NP):
            s = st_ref[p]
            sto_ref[0, 2 * p] = s[:, 0:N]
            sto_ref[0, 2 * p + 1] = s[:, N:2 * N]


def rwkv_mixer(proj, shift_state, rw_state, mu, w0, w2, a0, a2, g2, k_k, k_a, r_k, ln_g, ln_b):
    B, T, _ = proj.shape
    L = RW_L
    assert T % L == 0
    H, N = RW_HEADS, RW_N
    shift0 = jnp.pad(shift_state[:, None, :], ((0, 0), (SUBLANES - 1, 0), (0, 0)))
    w2p = jnp.pad(w2, ((0, LANES - w2.shape[0]), (0, 0)))
    a2p = jnp.pad(a2, ((LANES - a2.shape[0], 0), (0, 0)))
    vec = lambda w: pl.BlockSpec((1, w), lambda b, c: (0, 0))
    mat = pl.BlockSpec((LANES, D), lambda b, c: (0, 0))
    st_spec = pl.BlockSpec((1, H, N, N), lambda b, c: (b, 0, 0, 0))
    sh_spec = pl.BlockSpec((1, SUBLANES, RW_COLS), lambda b, c: (b, 0, 0))
    rows = pltpu.VMEM((H // 2, L, 2 * N), F32)
    cols = pltpu.VMEM((H // 2, N, 2 * L), F32)
    y, shifto, sto = pl.pallas_call(
        _rwkv_kernel,
        grid=(B, T // L),
        in_specs=[pl.BlockSpec((1, L, 3 * D), lambda b, c: (b, c, 0)),
                  pl.BlockSpec((1, L, RW_LORA), lambda b, c: (b, c, ODD_LORA_BLOCK)), sh_spec, st_spec,
                  vec(RW_COLS), vec(D), mat, vec(D), mat, mat, vec(D), vec(D), vec(D), vec(D), vec(D)],
        out_specs=[pl.BlockSpec((1, L, D), lambda b, c: (b, c, 0)), sh_spec, st_spec],
        out_shape=[jax.ShapeDtypeStruct((B, T, D), F32),
                   jax.ShapeDtypeStruct((B, SUBLANES, RW_COLS), F32),
                   jax.ShapeDtypeStruct((B, H, N, N), F32)],
        scratch_shapes=[pltpu.VMEM((SUBLANES + L, RW_COLS), F32), pltpu.VMEM((H // 2, N, 2 * N), F32),
                        rows, rows, rows, rows, rows, cols, cols, pltpu.VMEM((L, D), F32)],
        compiler_params=_cparams(("arbitrary", "arbitrary"), 48),
        name="rwkv_mixer",
    )(proj, proj, shift0, rw_state, mu.reshape(1, RW_COLS), w0.reshape(1, D), w2p, a0.reshape(1, D), a2p, g2,
      k_k.reshape(1, D), k_a.reshape(1, D), r_k.reshape(1, D), ln_g.reshape(1, D), ln_b.reshape(1, D))
    return y, shifto[:, SUBLANES - 1], sto


def _logf_kernel(fl_ref, bias_ref, o_ref):
    x = fl_ref[0] + bias_ref[...]
    lf = jnp.minimum(x, 0.0) - jnp.log1p(jnp.exp(-jnp.abs(x)))
    o_ref[0] = lf[:, 0:FOX_HEADS]


def fox_logf(proj, f_bias):
    B, T, _ = proj.shape
    tt = min(T, 512)
    return pl.pallas_call(
        _logf_kernel,
        grid=(B, T // tt),
        in_specs=[pl.BlockSpec((1, tt, LANES), lambda b, i: (b, i, ODD_FL_BLOCK)),
                  pl.BlockSpec((1, LANES), lambda b, i: (0, 0))],
        out_specs=pl.BlockSpec((1, tt, FOX_HEADS), lambda b, i: (b, i, 0)),
        out_shape=jax.ShapeDtypeStruct((B, T, FOX_HEADS), F32),
        compiler_params=_cparams(("arbitrary", "arbitrary")),
        name="fox_logf",
    )(proj, jnp.pad(f_bias.reshape(1, FOX_HEADS), ((0, 0), (0, LANES - FOX_HEADS))))


def _cumsum_kernel(x_ref, o_ref, carry_ref):
    ts = x_ref.shape[1]

    @pl.when(pl.program_id(1) == 0)
    def _():
        carry_ref[...] = jnp.zeros(carry_ref.shape, F32)

    row = lax.broadcasted_iota(jnp.int32, (ts, ts), 0)
    col = lax.broadcasted_iota(jnp.int32, (ts, ts), 1)
    cum = _mm_hi((col <= row).astype(F32), x_ref[0]) + carry_ref[...]
    o_ref[0] = cum
    carry_ref[...] = cum[ts - 1:ts, :]


def time_cumsum(x):
    B, S, H = x.shape
    ts = 512 if S % 512 == 0 else LANES
    assert S % ts == 0
    return pl.pallas_call(
        _cumsum_kernel,
        grid=(B, S // ts),
        in_specs=[pl.BlockSpec((1, ts, H), lambda b, i: (b, i, 0))],
        out_specs=pl.BlockSpec((1, ts, H), lambda b, i: (b, i, 0)),
        out_shape=jax.ShapeDtypeStruct((B, S, H), F32),
        scratch_shapes=[pltpu.VMEM((1, H), F32)],
        compiler_params=_cparams(("arbitrary", "arbitrary")),
        name="time_cumsum",
    )(x)


def _aug_placement(mode):
    h = jnp.arange(LANES)[:, None]
    c = jnp.arange(FOX_AW)[None, :]
    base = h * LANES + FOX_HD
    valid = h < FOX_HEADS
    term_off = 0 if mode == 'q' else 3
    sign = 1.0 if mode == 'q' else -1.0
    place = jnp.stack([jnp.where(valid & (c == base + term_off + i), sign, 0.0) for i in range(3)]).astype(BF16)
    lane = jnp.arange(FOX_AW) % LANES
    if mode == 'q':
        ones = (lane >= FOX_HD + 3) & (lane < FOX_HD + 6)
    elif mode == 'k':
        ones = (lane >= FOX_HD) & (lane < FOX_HD + 3)
    else:
        ones = lane == FOX_HD
    return place, jnp.where(ones, 1.0, 0.0).reshape(1, FOX_AW).astype(F32)


def _fox_aug_kernel(x_ref, cum_ref, place_ref, ones_ref, o_ref, *, mode):
    tt = x_ref.shape[1]
    x = x_ref[0]
    if mode == 'q':
        x = x * (FOX_HD ** -0.5 * LOG2E)
    aug = ones_ref[...]
    if mode != 'v':
        c = cum_ref[0] * LOG2E
        c1 = c.astype(BF16)
        r1 = c - c1.astype(F32)
        c2 = r1.astype(BF16)
        c3 = (r1 - c2.astype(F32)).astype(BF16)
        aug = aug + (jnp.dot(c1, place_ref[0], preferred_element_type=F32)
                     + jnp.dot(c2, place_ref[1], preferred_element_type=F32)
                     + jnp.dot(c3, place_ref[2], preferred_element_type=F32))
    else:
        aug = jnp.broadcast_to(aug, (tt, FOX_AW))
    lo = lax.broadcasted_iota(jnp.int32, (tt, LANES), 1) < FOX_HD
    for m in range(FOX_HEADS // 2):
        x_m = x[:, LANES * m:LANES * (m + 1)]
        even = jnp.where(lo, x_m, aug[:, 2 * LANES * m:2 * LANES * m + LANES])
        odd = jnp.where(lo, pltpu.roll(x_m, FOX_HD, 1), aug[:, 2 * LANES * m + LANES:2 * LANES * (m + 1)])
        o_ref[0, :, 2 * LANES * m:2 * LANES * (m + 1)] = jnp.concatenate([even, odd], axis=1).astype(BF16)


def fox_augment(arr, col_block, cum, mode):
    B, S, _ = arr.shape
    tt = min(S, 256)
    if S % tt:
        tt = LANES
    place, ones = _aug_placement(mode)
    out_spec = pl.BlockSpec((1, tt, FOX_AW), lambda b, i: (b, i, 0))
    out_shape = jax.ShapeDtypeStruct((B, S, FOX_AW), BF16)
    return pl.pallas_call(
        functools.partial(_fox_aug_kernel, mode=mode),
        grid=(B, S // tt),
        in_specs=[pl.BlockSpec((1, tt, D), lambda b, i: (b, i, col_block)),
                  pl.BlockSpec((1, tt, LANES), lambda b, i: (b, i, 0)),
                  pl.BlockSpec((3, LANES, FOX_AW), lambda b, i: (0, 0, 0)),
                  pl.BlockSpec((1, FOX_AW), lambda b, i: (0, 0))],
        out_specs=out_spec,
        out_shape=out_shape,
        compiler_params=_cparams(("arbitrary", "arbitrary")),
        name="fox_augment_" + mode,
    )(arr, cum, place, ones)


def _fox_kernel(i_ref, j_ref, qa_ref, ka_ref, va_ref, o_ref, m_ref, acc_ref, *, past, tq, tk):
    s_idx = pl.program_id(2)
    i = i_ref[s_idx]
    j = j_ref[s_idx]
    first_q = past + i * tq

    @pl.when(j == 0)
    def _():
        m_ref[...] = jnp.full(m_ref.shape, NEG_INF, F32)
        acc_ref[...] = jnp.zeros(acc_ref.shape, F32)

    kc = FOX_KC if tk % FOX_KC == 0 else tk
    diagonal_only = past == 0 and tq == tk

    def tile(masked):
        for kh in range(tk // kc):
            ks = slice(kc * kh, kc * (kh + 1))
            r0 = kc * kh if (masked and diagonal_only) else 0
            nr = tq - r0
            scores = []
            for hh in range(2):
                hs = slice(LANES * hh, LANES * (hh + 1))
                s = lax.dot_general(qa_ref[0, r0:, hs], ka_ref[0, ks, hs], (((1,), (1,)), ((), ())),
                                    preferred_element_type=F32)
                if masked:
                    diff = (lax.broadcasted_iota(jnp.int32, (nr, kc), 1)
                            - lax.broadcasted_iota(jnp.int32, (nr, kc), 0))
                    s = jnp.where(diff <= first_q + r0 - j * tk - kc * kh, s, NEG_INF)
                scores.append(s)
            for hh, s in enumerate(scores):
                hs = slice(LANES * hh, LANES * (hh + 1))
                chunks = [s[:, LANES * c:LANES * (c + 1)] for c in range(kc // LANES)]
                smax = functools.reduce(jnp.maximum, chunks)
                m_old = m_ref[hh, r0:, :]
                m_new = jnp.maximum(m_old, jnp.max(smax, axis=1, keepdims=True))
                pm = jnp.concatenate([jnp.exp2(ch - m_new) for ch in chunks], axis=1)
                acc_ref[hh, r0:, :] = (jnp.exp2(m_old - m_new) * acc_ref[hh, r0:, :]
                                       + jnp.dot(pm.astype(BF16), va_ref[0, ks, hs], preferred_element_type=F32))
                m_ref[hh, r0:, :] = m_new

    unmasked = (j + 1) * tk - 1 <= first_q

    @pl.when(unmasked)
    def _():
        tile(False)

    @pl.when(jnp.logical_not(unmasked))
    def _():
        tile(True)

    @pl.when(j == (first_q + tq - 1) // tk)
    def _():
        lo = lax.broadcasted_iota(jnp.int32, (tq, LANES), 1) < FOX_HD
        a0 = acc_ref[0]
        a1 = acc_ref[1]
        o0 = a0 / a0[:, FOX_HD:FOX_HD + 1]
        o1 = a1 / a1[:, FOX_HD:FOX_HD + 1]
        o_ref[0] = jnp.where(lo, o0, pltpu.roll(o1, FOX_HD, 1))


def fox_attention(qa, ka, va, past, T):
    B, S, _ = ka.shape
    tq = min(T, FOX_TQ)
    tk = FOX_TK if S % FOX_TK == 0 else S
    nq = T // tq
    pairs = [(i, j) for i in range(nq) for j in range((past + (i + 1) * tq - 1) // tk + 1)]
    i_idx = jnp.asarray([p[0] for p in pairs], jnp.int32)
    j_idx = jnp.asarray([p[1] for p in pairs], jnp.int32)
    grid_spec = pltpu.PrefetchScalarGridSpec(
        num_scalar_prefetch=2,
        grid=(B, FOX_HEADS // 2, len(pairs)),
        in_specs=[
            pl.BlockSpec((1, tq, 2 * LANES), lambda b, p, s, ii, jj: (b, ii[s], p)),
            pl.BlockSpec((1, tk, 2 * LANES), lambda b, p, s, ii, jj: (b, jj[s], p)),
            pl.BlockSpec((1, tk, 2 * LANES), lambda b, p, s, ii, jj: (b, jj[s], p)),
        ],
        out_specs=pl.BlockSpec((1, tq, LANES), lambda b, p, s, ii, jj: (b, ii[s], p)),
        scratch_shapes=[pltpu.VMEM((2, tq, LANES), F32), pltpu.VMEM((2, tq, LANES), F32)],
    )
    return pl.pallas_call(
        functools.partial(_fox_kernel, past=past, tq=tq, tk=tk),
        grid_spec=grid_spec,
        out_shape=jax.ShapeDtypeStruct((B, T, D), F32),
        compiler_params=_cparams(("arbitrary", "arbitrary", "arbitrary"), 48),
        name="fox_attention",
    )(i_idx, j_idx, qa, ka, va)


def _trunk(x, mods, prm, wts, conv_st, ssd_st, hg_st, shift_st, rw_st, fk_past, fv_past, fl_past):
    B, T, _ = x.shape
    n = B * T
    sh1, sc1, gt1, sh2, sc2, gt2 = mods[0]
    proj = norm_inproj(x, prm['norm_mix_g'][0], sc1, sh1, wts['even_in'])
    ya, n_conv, n_ssd = ssd_mixer(proj, conv_st, ssd_st, prm['ssd_conv_w'][0], prm['ssd_conv_b'][0],
                                  prm['ssd_dt_bias'][0], prm['ssd_a_log'][0], prm['ssd_d'][0],
                                  prm['ssd_norm_g'][0])
    yb, n_hg = hgrn_mixer(proj, hg_st, wts['hgrn_lb'], prm['hgrn_norm_g'][0])
    x, hn = mix_outproj(ya, yb, wts['even_out'], x, gt1, prm['norm_ffn_g'][0], sc2, sh2)
    hn2 = hn.reshape(n, D)
    gates = moe_router(hn2, prm['moe_router_w'][0], prm['moe_router_bias'][0])
    ffn = moe_experts(hn2, gates, 0, prm['moe_w_gate'], prm['moe_w_up'], prm['moe_w_down'],
                      prm['moe_sh_gate'], prm['moe_sh_up'], prm['moe_sh_down'])
    x = residual(x, ffn, gt2, prm['final_norm_g'], False)

    sh1, sc1, gt1, sh2, sc2, gt2 = mods[1]
    proj = norm_inproj(x, prm['norm_mix_g'][1], sc1, sh1, wts['odd_in'])
    yc, n_shift, n_rw = rwkv_mixer(proj, shift_st, rw_st, prm['rwkv_mu'][0], prm['rwkv_w0'][0], prm['rwkv_w2'][0],
                                   prm['rwkv_a0'][0], prm['rwkv_a2'][0], prm['rwkv_g2'][0], prm['rwkv_k_k'][0],
                                   prm['rwkv_k_a'][0], prm['rwkv_r_k'][0], prm['rwkv_ln_g'][0],
                                   prm['rwkv_ln_b'][0])
    logf = fox_logf(proj, prm['fox_f_bias'][0])
    kf = proj[:, :, ODD_K_BLOCK * D:(ODD_K_BLOCK + 1) * D]
    vf = proj[:, :, ODD_V_BLOCK * D:(ODD_V_BLOCK + 1) * D]
    lane_pad = lambda a: jnp.pad(a, ((0, 0), (0, 0), (0, LANES - FOX_HEADS)))
    if fk_past is None:
        past = 0
        cum = time_cumsum(lane_pad(logf))
        ka = fox_augment(proj, ODD_K_BLOCK, cum, 'k')
        va = fox_augment(proj, ODD_V_BLOCK, cum, 'v')
    else:
        past = fk_past.shape[1]
        pad = (-(past + T)) % LANES
        lf_all = jnp.concatenate([fl_past, logf, jnp.zeros((B, pad, FOX_HEADS), F32)], axis=1)
        cum = time_cumsum(lane_pad(lf_all))
        k_arr = jnp.concatenate([fk_past.reshape(B, past, D), kf, jnp.zeros((B, pad, D), F32)], axis=1)
        v_arr = jnp.concatenate([fv_past.reshape(B, past, D), vf, jnp.zeros((B, pad, D), F32)], axis=1)
        ka = fox_augment(k_arr, 0, cum, 'k')
        va = fox_augment(v_arr, 0, cum, 'v')
    qa = fox_augment(proj, ODD_Q_BLOCK, cum[:, past:past + T], 'q')
    yd = fox_attention(qa, ka, va, past, T)
    x, hn = mix_outproj(yc, yd, wts['odd_out'], x, gt1, prm['norm_ffn_g'][1], sc2, sh2)
    hn2 = hn.reshape(n, D)
    gates = moe_router(hn2, prm['moe_router_w'][1], prm['moe_router_bias'][1])
    ffn = moe_experts(hn2, gates, 1, prm['moe_w_gate'], prm['moe_w_up'], prm['moe_w_down'],
                      prm['moe_sh_gate'], prm['moe_sh_up'], prm['moe_sh_down'])
    y = residual(x, ffn, gt2, prm['final_norm_g'], True)
    hs = (B, T, FOX_HEADS, FOX_HD)
    return (y, n_conv[None], n_ssd[None], n_hg[None], n_shift[None], n_rw[None],
            kf.reshape(hs)[None], vf.reshape(hs)[None], logf[None])


def _prep_weights(prm):
    ein = prm['even_in_w'][0]
    z, xbc, dt, hq, hf, hi, hgate = jnp.split(ein, [1024, 2304, 2320, 3344, 4368, 5392], axis=1)
    even_in = jnp.concatenate([z, hq, hf, hi, hgate, xbc, dt, jnp.zeros((D, LANES - SSD_HEADS), F32)], axis=1)
    rkv, lora, qkv, fl = jnp.split(prm['odd_in_w'][0], [3 * D, RW_COLS, RW_COLS + 3 * D], axis=1)
    odd_in = jnp.concatenate([rkv, qkv, lora, fl, jnp.zeros((D, LANES - FOX_HEADS), F32)], axis=1)
    lb = jnp.cumsum(jax.nn.softmax(prm['hgrn_lb_logits'].astype(F32), axis=0), axis=0)[0]
    return {
        'even_in': even_in.astype(BF16),
        'odd_in': odd_in.astype(BF16),
        'even_out': prm['even_out_w'][0].astype(BF16),
        'odd_out': prm['odd_out_w'][0].astype(BF16),
        'hgrn_lb': lb,
    }


def kernel(x_prompt, x_sample, c_prompt, c_sample, state_ssd_conv, state_ssd, state_hgrn, state_rwkv_shift, state_rwkv, cache_fox_k, cache_fox_v, cache_fox_logf, ada_w, ada_b, norm_mix_g, norm_ffn_g, final_norm_g, even_in_w, even_out_w, ssd_conv_w, ssd_conv_b, ssd_dt_bias, ssd_a_log, ssd_d, ssd_norm_g, hgrn_lb_logits, hgrn_norm_g, odd_in_w, odd_out_w, rwkv_mu, rwkv_w0, rwkv_w2, rwkv_a0, rwkv_a2, rwkv_g2, rwkv_k_k, rwkv_k_a, rwkv_r_k, rwkv_ln_g, rwkv_ln_b, fox_f_bias, moe_router_w, moe_router_bias, moe_w_gate, moe_w_up, moe_w_down, moe_sh_gate, moe_sh_up, moe_sh_down):
    prm = dict(ada_w=ada_w, ada_b=ada_b, norm_mix_g=norm_mix_g, norm_ffn_g=norm_ffn_g, final_norm_g=final_norm_g,
               even_in_w=even_in_w, even_out_w=even_out_w, ssd_conv_w=ssd_conv_w, ssd_conv_b=ssd_conv_b,
               ssd_dt_bias=ssd_dt_bias, ssd_a_log=ssd_a_log, ssd_d=ssd_d, ssd_norm_g=ssd_norm_g,
               hgrn_lb_logits=hgrn_lb_logits, hgrn_norm_g=hgrn_norm_g,
               odd_in_w=odd_in_w, odd_out_w=odd_out_w, rwkv_mu=rwkv_mu, rwkv_w0=rwkv_w0, rwkv_w2=rwkv_w2,
               rwkv_a0=rwkv_a0, rwkv_a2=rwkv_a2, rwkv_g2=rwkv_g2, rwkv_k_k=rwkv_k_k, rwkv_k_a=rwkv_k_a,
               rwkv_r_k=rwkv_r_k, rwkv_ln_g=rwkv_ln_g, rwkv_ln_b=rwkv_ln_b, fox_f_bias=fox_f_bias,
               moe_router_w=moe_router_w, moe_router_bias=moe_router_bias, moe_w_gate=moe_w_gate,
               moe_w_up=moe_w_up, moe_w_down=moe_w_down, moe_sh_gate=moe_sh_gate, moe_sh_up=moe_sh_up,
               moe_sh_down=moe_sh_down)
    wts = _prep_weights(prm)
    bp, bs = x_prompt.shape[0], x_sample.shape[0]
    depth = ada_w.shape[0]
    c_all = jnp.concatenate([c_prompt, c_sample], axis=0)
    rows = c_all.shape[0]
    c_all = jnp.pad(c_all, ((0, (-rows) % SUBLANES), (0, 0)))
    mod = ada_mod(c_all, ada_w, ada_b)

    def mods_for(lo, hi):
        return [tuple(m[:, None, :] for m in jnp.split(mod[l, lo:hi], 6, axis=-1)) for l in range(depth)]

    z = lambda *s: jnp.zeros(s, F32)
    out_p = _trunk(x_prompt, mods_for(0, bp), prm, wts,
                   z(bp, SSD_CONV - 1, SSD_CONV_DIM), z(bp, SSD_HEADS, SSD_HD, SSD_STATE),
                   z(bp, HG_HEADS, HG_DK, HG_DV), z(bp, RW_COLS), z(bp, RW_HEADS, RW_N, RW_N),
                   None, None, None)
    out_s = _trunk(x_sample, mods_for(bp, bp + bs), prm, wts,
                   state_ssd_conv[0], state_ssd[0], state_hgrn[0], state_rwkv_shift[0], state_rwkv[0],
                   cache_fox_k[0], cache_fox_v[0], cache_fox_logf[0])
    return (out_p[0], out_s[0]) + tuple(out_p[1:]) + tuple(out_s[1:])
```
